```python
import math
import jax, jax.numpy as jnp
from jax import lax
import numpy as np

D_MODEL = 2048
BATCH = 4
SEQ = 4096
DEPTH = 1
DEC_BATCH = 16
DEC_SEQ = 2048
PAST_LEN = 128

D_HY = D_MODEL // 2
HY_ORDER = 2
HY_EMB = 33
HY_BANDS = (HY_EMB - 1) // 2
HY_FFN = 64
HY_FAST_DECAY_PCT = 0.3
HY_SLOW_DECAY_PCT = 1.5
HY_DECAY_TARGET = 1e-2

D_RW = D_MODEL // 2
RW_HEAD = 64
RW_HEADS = D_RW // RW_HEAD
RW_DECAY_LORA = 64
RW_AAA_LORA = 64
RW_GATE_LORA = 160
RW_GN_EPS = 64e-5

MEM_TOKENS = 256
D_CA = D_MODEL // 2
CA_HEADS = 4
CA_HEAD = D_CA // CA_HEADS

N_BRANCH = 3
N_EXPERTS = 16
EC_CAPACITY = 2
D_EXPERT = 2 * D_MODEL
LN_EPS = 1e-5
DN_ALPHA = (2 * DEPTH) ** 0.25
DN_BETA = (8 * DEPTH) ** -0.25

OFF_HY = 0
OFF_RW = OFF_HY + 3 * D_HY
OFF_LORA = OFF_RW + 3 * D_RW
N_LORA = 2 * RW_DECAY_LORA + 2 * RW_AAA_LORA + RW_GATE_LORA
OFF_Q = OFF_LORA + N_LORA
N_SHORT = OFF_Q
OFF_GATE = OFF_Q + D_CA
N_IN = OFF_GATE + N_BRANCH * D_MODEL

kernel_name = 'hybrid_hyena_rwkv7_ec_encoder'


def _layer_norm(x, g, b):
    xf = x.astype(jnp.float32)
    mu = jnp.mean(xf, axis=-1, keepdims=True)
    var = jnp.mean(jnp.square(xf - mu), axis=-1, keepdims=True)
    return ((xf - mu) * lax.rsqrt(var + LN_EPS) * g.astype(jnp.float32) + b.astype(jnp.float32)).astype(x.dtype)


def _short_conv3(u, w):
    up = jnp.pad(u, ((0, 0), (1, 1), (0, 0)))
    return w[0] * up[:, :-2] + w[1] * up[:, 1:-1] + w[2] * up[:, 2:]


def _hyena_filters(L, w1, b1, w2, b2, w3, b3, w4, sin_freq):
    f32 = jnp.float32
    t = jnp.linspace(0.0, 1.0, L, dtype=f32)[:, None]
    ang = 2.0 * math.pi * jnp.arange(L, dtype=f32)[:, None] / L
    bands = jnp.linspace(1e-4, HY_BANDS - 1, HY_BANDS, dtype=f32)[None]
    z = jnp.concatenate([t, jnp.cos(bands * ang), -jnp.sin(bands * ang)], axis=-1)
    sf = sin_freq.astype(f32)
    h = jnp.sin(sf[0] * (z @ w1.astype(f32) + b1.astype(f32)))
    h = jnp.sin(sf[1] * (h @ w2.astype(f32) + b2.astype(f32)))
    h = jnp.sin(sf[2] * (h @ w3.astype(f32) + b3.astype(f32)))
    h = (h @ w4.astype(f32)).reshape(L, HY_ORDER, 2, D_HY)
    deltas = jnp.abs(jnp.linspace(math.log(HY_DECAY_TARGET) / HY_SLOW_DECAY_PCT,
                                  math.log(HY_DECAY_TARGET) / HY_FAST_DECAY_PCT, D_HY, dtype=f32))
    window = jnp.exp(-t * deltas[None])
    return h * window[:, None, None, :]


def _bidir_fftconv(z, h_fwd, h_bwd, dbias):
    L = z.shape[1]
    kern = jnp.concatenate([h_fwd, jnp.zeros_like(h_fwd[:1]), h_bwd[:0:-1]], axis=0)
    kf = jnp.fft.rfft(kern, axis=0)
    zf = jnp.fft.rfft(z, n=2 * L, axis=1)
    y = jnp.fft.irfft(zf * kf[None], n=2 * L, axis=1)[:, :L]
    return y + z * dbias


def _hyena(u, filt, dbias):
    v, x1, x2 = jnp.split(u.astype(jnp.float32), 3, axis=-1)
    db = dbias.astype(jnp.float32)
    z = x1 * _bidir_fftconv(v, filt[:, 0, 0], filt[:, 0, 1], db[0])
    return x2 * _bidir_fftconv(z, filt[:, 1, 0], filt[:, 1, 1], db[1])


def _wkv7_scan(r, decay, k, v, kk, kka, reverse):
    B, _, H, N = r.shape

    def step(S, inp):
        r_t, w_t, k_t, v_t, kk_t, kka_t = inp
        s_kk = jnp.einsum('bhvk,bhk->bhv', S, kk_t)
        S = S * w_t[:, :, None, :] - s_kk[..., None] * kka_t[:, :, None, :] + v_t[..., None] * k_t[:, :, None, :]
        return S, jnp.einsum('bhvk,bhk->bhv', S, r_t)

    xs = tuple(jnp.swapaxes(t, 0, 1) for t in (r, decay, k, v, kk, kka))
    s0 = jnp.zeros((B, H, N, N), jnp.float32)
    _, y = lax.scan(step, s0, xs, reverse=reverse)
    return jnp.swapaxes(y, 0, 1)


def _heads(t):
    return t.reshape(t.shape[:-1] + (RW_HEADS, RW_HEAD))


def _rwkv7(u, lo, w0, w2, a0, a2, g2, k_k, k_a, r_k, ln_w, ln_b):
    B, L, _ = u.shape
    f32 = jnp.float32
    r, k, v = jnp.split(u.astype(f32), 3, axis=-1)
    lo = lo.astype(f32)
    nw, na = 2 * RW_DECAY_LORA, 2 * RW_AAA_LORA
    xw = lo[..., :nw].reshape(B, L, 2, RW_DECAY_LORA)
    xa = lo[..., nw:nw + na].reshape(B, L, 2, RW_AAA_LORA)
    xg = lo[..., nw + na:]
    logw = -jax.nn.softplus(-(w0.astype(f32) + jnp.einsum('bldr,drc->bldc', jnp.tanh(xw), w2.astype(f32)))) - 0.5
    decay = jnp.exp(-jnp.exp(logw))
    a = jax.nn.sigmoid(a0.astype(f32) + jnp.einsum('bldr,drc->bldc', xa, a2.astype(f32)))
    kk = _heads(k * k_k.astype(f32))
    kk = kk / jnp.maximum(jnp.sqrt(jnp.sum(kk * kk, axis=-1, keepdims=True)), 1e-12)
    kdir = k[:, :, None] * (1.0 + (a - 1.0) * k_a.astype(f32))
    rh, vh = _heads(r), _heads(v)
    dh, kh, ah = _heads(decay), _heads(kdir), _heads(a)
    y = (_wkv7_scan(rh, dh[:, :, 0], kh[:, :, 0], vh, kk, kk * ah[:, :, 0], False)
         + _wkv7_scan(rh, dh[:, :, 1], kh[:, :, 1], vh, kk, kk * ah[:, :, 1], True))
    mu = jnp.mean(y, axis=-1, keepdims=True)
    var = jnp.mean(jnp.square(y - mu), axis=-1, keepdims=True)
    y = ((y - mu) * lax.rsqrt(var + RW_GN_EPS)).reshape(B, L, D_RW) * ln_w.astype(f32) + ln_b.astype(f32)
    bonus = jnp.sum(rh[:, :, None] * kh * r_k.astype(f32), axis=-1, keepdims=True) * vh[:, :, None]
    y = y + jnp.sum(bonus, axis=2).reshape(B, L, D_RW)
    g = jax.nn.sigmoid(xg) @ g2.astype(f32)
    return y * g


def _memory_attention(q, mem, w_kv):
    B, L, _ = q.shape
    M = mem.shape[1]
    kv = mem @ w_kv
    k = kv[..., :D_CA].reshape(B, M, CA_HEADS, CA_HEAD)
    v = kv[..., D_CA:].reshape(B, M, CA_HEADS, CA_HEAD)
    qh = q.reshape(B, L, CA_HEADS, CA_HEAD)
    s = jnp.einsum('blhd,bmhd->bhlm', qh, k).astype(jnp.float32) * (CA_HEAD ** -0.5)
    prob = jax.nn.softmax(s, axis=-1).astype(v.dtype)
    return jnp.einsum('bhlm,bmhd->blhd', prob, v).reshape(B, L, D_CA)


def _expert_choice_moe(x, w_router, w_gate, w_up, w_down):
    B, L, D = x.shape
    T = B * L
    cap = (EC_CAPACITY * T) // N_EXPERTS
    xt = x.reshape(T, D)
    aff = jax.nn.softmax((xt @ w_router).astype(jnp.float32), axis=-1)
    gval, idx = lax.top_k(aff.T, cap)

    def expert(args):
        wg, wu, wd, ids, gv = args
        xe = xt[ids]
        h = jax.nn.silu(xe @ wg) * (xe @ wu)
        return (h @ wd) * gv[:, None].astype(x.dtype)

    out = lax.map(expert, (w_gate, w_up, w_down, idx, gval))
    y = jnp.zeros_like(xt).at[idx.reshape(-1)].add(out.reshape(-1, D))
    return y.reshape(B, L, D)


def _layer(x, mem, p):
    B, L, _ = x.shape
    dt = x.dtype
    proj = x @ p['w_in']
    sc = _short_conv3(proj[..., :N_SHORT], p['w_short'])
    filt = _hyena_filters(L, p['hy_w1'], p['hy_b1'], p['hy_w2'], p['hy_b2'], p['hy_w3'], p['hy_b3'],
                          p['hy_w4'], p['hy_sin_freq'])
    hy = _hyena(sc[..., OFF_HY:OFF_RW], filt, p['hy_dbias']).astype(dt) @ p['hy_w_o']
    rw = _rwkv7(sc[..., OFF_RW:OFF_LORA], sc[..., OFF_LORA:OFF_Q], p['rw_w0'], p['rw_w2'], p['rw_a0'],
                p['rw_a2'], p['rw_g2'], p['rw_k_k'], p['rw_k_a'], p['rw_r_k'], p['rw_ln_w'],
                p['rw_ln_b']).astype(dt) @ p['rw_w_o']
    ca = _memory_attention(proj[..., OFF_Q:OFF_GATE], mem, p['ca_w_kv']) @ p['ca_w_o']
    gates = jax.nn.sigmoid(proj[..., OFF_GATE:].astype(jnp.float32)).astype(dt).reshape(B, L, N_BRANCH, D_MODEL)
    merged = gates[:, :, 0] * hy + gates[:, :, 1] * rw + gates[:, :, 2] * ca
    x = _layer_norm(DN_ALPHA * x + merged @ p['w_out'], p['ln1_g'], p['ln1_b'])
    moe = _expert_choice_moe(x, p['moe_w_router'], p['moe_w_gate'], p['moe_w_up'], p['moe_w_down'])
    return _layer_norm(DN_ALPHA * x + moe, p['ln2_g'], p['ln2_b'])


def setup_inputs(seed: int = 0) -> dict:
    key = jax.random.key(seed)
    ks = iter(jax.random.split(key, 64))

    def nrm(shape, scale=1.0):
        return jax.random.normal(next(ks), shape, jnp.float32) * scale

    Dp = DEPTH
    col_scale = jnp.ones((N_IN,), jnp.float32)
    col_scale = col_scale.at[OFF_HY:OFF_HY + D_HY].set(DN_BETA).at[OFF_RW + 2 * D_RW:OFF_LORA].set(DN_BETA)
    kv_scale = jnp.ones((2 * D_CA,), jnp.float32).at[D_CA:].set(DN_BETA)
    ramp = -7.0 + 5.0 * (jnp.arange(D_RW, dtype=jnp.float32) / (D_RW - 1)) ** 0.85 + 0.5
    short_base = jnp.array([0.25, 0.5, 0.25], jnp.float32)[:, None]
    return {
        'x_prompt': nrm((BATCH, SEQ, D_MODEL)),
        'x_sample': nrm((DEC_BATCH, DEC_SEQ, D_MODEL)),
        'mem_prompt': nrm((BATCH, MEM_TOKENS, D_MODEL)),
        'mem_sample': nrm((DEC_BATCH, MEM_TOKENS, D_MODEL)),
        'w_in': nrm((Dp, D_MODEL, N_IN), D_MODEL ** -0.5) * col_scale,
        'w_short': short_base + nrm((Dp, 3, N_SHORT), 0.1),
        'hy_w1': nrm((Dp, HY_EMB, HY_FFN), HY_EMB ** -0.5),
        'hy_b1': nrm((Dp, HY_FFN), 0.1),
        'hy_w2': nrm((Dp, HY_FFN, HY_FFN), HY_FFN ** -0.5),
        'hy_b2': nrm((Dp, HY_FFN), 0.1),
        'hy_w3': nrm((Dp, HY_FFN, HY_FFN), HY_FFN ** -0.5),
        'hy_b3': nrm((Dp, HY_FFN), 0.1),
        'hy_w4': nrm((Dp, HY_FFN, HY_ORDER * 2 * D_HY), 0.005),
        'hy_sin_freq': 1.0 + nrm((Dp, 3, HY_FFN), 0.1),
        'hy_dbias': nrm((Dp, HY_ORDER, D_HY), 1.0),
        'hy_w_o': nrm((Dp, D_HY, D_MODEL), D_HY ** -0.5 * DN_BETA),
        'rw_w0': ramp + nrm((Dp, 2, D_RW), 0.1),
        'rw_w2': nrm((Dp, 2, RW_DECAY_LORA, D_RW), 0.1),
        'rw_a0': nrm((Dp, 2, D_RW), 0.1),
        'rw_a2': nrm((Dp, 2, RW_AAA_LORA, D_RW), 0.1),
        'rw_g2': nrm((Dp, RW_GATE_LORA, D_RW), RW_GATE_LORA ** -0.5),
        'rw_k_k': 0.85 + nrm((Dp, D_RW), 0.05),
        'rw_k_a': 1.0 + nrm((Dp, D_RW), 0.05),
        'rw_r_k': nrm((Dp, 2, RW_HEADS, RW_HEAD), 0.1),
        'rw_ln_w': 1.0 + nrm((Dp, D_RW), 0.05),
        'rw_ln_b': nrm((Dp, D_RW), 0.02),
        'rw_w_o': nrm((Dp, D_RW, D_MODEL), D_RW ** -0.5 * DN_BETA),
        'ca_w_kv': nrm((Dp, D_MODEL, 2 * D_CA), D_MODEL ** -0.5) * kv_scale,
        'ca_w_o': nrm((Dp, D_CA, D_MODEL), D_CA ** -0.5 * DN_BETA),
        'w_out': nrm((Dp, D_MODEL, D_MODEL), D_MODEL ** -0.5 * DN_BETA),
        'ln1_g': 1.0 + nrm((Dp, D_MODEL), 0.05),
        'ln1_b': nrm((Dp, D_MODEL), 0.02),
        'moe_w_router': nrm((Dp, D_MODEL, N_EXPERTS), D_MODEL ** -0.5),
        'moe_w_gate': nrm((Dp, N_EXPERTS, D_MODEL, D_EXPERT), D_MODEL ** -0.5),
        'moe_w_up': nrm((Dp, N_EXPERTS, D_MODEL, D_EXPERT), D_MODEL ** -0.5),
        'moe_w_down': nrm((Dp, N_EXPERTS, D_EXPERT, D_MODEL), D_EXPERT ** -0.5 * DN_BETA),
        'ln2_g': 1.0 + nrm((Dp, D_MODEL), 0.05),
        'ln2_b': nrm((Dp, D_MODEL), 0.02),
    }


def reference(x_prompt, x_sample, mem_prompt, mem_sample, w_in, w_short, hy_w1, hy_b1, hy_w2, hy_b2,
              hy_w3, hy_b3, hy_w4, hy_sin_freq, hy_dbias, hy_w_o, rw_w0, rw_w2, rw_a0, rw_a2, rw_g2,
              rw_k_k, rw_k_a, rw_r_k, rw_ln_w, rw_ln_b, rw_w_o, ca_w_kv, ca_w_o, w_out, ln1_g, ln1_b,
              moe_w_router, moe_w_gate, moe_w_up, moe_w_down, ln2_g, ln2_b):
    params = dict(w_in=w_in, w_short=w_short, hy_w1=hy_w1, hy_b1=hy_b1, hy_w2=hy_w2, hy_b2=hy_b2,
                  hy_w3=hy_w3, hy_b3=hy_b3, hy_w4=hy_w4, hy_sin_freq=hy_sin_freq, hy_dbias=hy_dbias,
                  hy_w_o=hy_w_o, rw_w0=rw_w0, rw_w2=rw_w2, rw_a0=rw_a0, rw_a2=rw_a2, rw_g2=rw_g2,
                  rw_k_k=rw_k_k, rw_k_a=rw_k_a, rw_r_k=rw_r_k, rw_ln_w=rw_ln_w, rw_ln_b=rw_ln_b,
                  rw_w_o=rw_w_o, ca_w_kv=ca_w_kv, ca_w_o=ca_w_o, w_out=w_out, ln1_g=ln1_g, ln1_b=ln1_b,
                  moe_w_router=moe_w_router, moe_w_gate=moe_w_gate, moe_w_up=moe_w_up,
                  moe_w_down=moe_w_down, ln2_g=ln2_g, ln2_b=ln2_b)
    y_prompt, y_sample = x_prompt, x_sample
    for i in range(DEPTH):
        lp = {name: arr[i] for name, arr in params.items()}
        y_prompt = _layer(y_prompt, mem_prompt, lp)
        y_sample = _layer(y_sample, mem_sample, lp)
    return (y_prompt, y_sample)
```

```python
import functools
import math

import jax
import jax.numpy as jnp
from jax import lax
from jax.experimental import pallas as pl
from jax.experimental.pallas import tpu as pltpu

D_MODEL = 2048
DEPTH = 1
D_HY = D_MODEL // 2
HY_ORDER = 2
HY_EMB = 33
HY_BANDS = (HY_EMB - 1) // 2
HY_FFN = 64
HY_FAST_DECAY_PCT = 0.3
HY_SLOW_DECAY_PCT = 1.5
HY_DECAY_TARGET = 1e-2
D_RW = D_MODEL // 2
RW_HEAD = 64
RW_HEADS = D_RW // RW_HEAD
RW_DECAY_LORA = 64
RW_AAA_LORA = 64
RW_GATE_LORA = 160
RW_GN_EPS = 64e-5
D_CA = D_MODEL // 2
CA_HEADS = 4
CA_HEAD = D_CA // CA_HEADS
N_BRANCH = 3
N_EXPERTS = 16
EC_CAPACITY = 2
D_EXPERT = 2 * D_MODEL
LN_EPS = 1e-5
DN_ALPHA = (2 * DEPTH) ** 0.25

OFF_HY = 0
OFF_RW = OFF_HY + 3 * D_HY
OFF_LORA = OFF_RW + 3 * D_RW
N_LORA = 2 * RW_DECAY_LORA + 2 * RW_AAA_LORA + RW_GATE_LORA
OFF_Q = OFF_LORA + N_LORA
OFF_GATE = OFF_Q + D_CA
N_IN = OFF_GATE + N_BRANCH * D_MODEL

LANES = 128
N_LORA_PAD = -(-N_LORA // LANES) * LANES
P_HY = 0
P_RW = P_HY + 3 * D_HY
P_LORA = P_RW + 3 * D_RW
P_Q = P_LORA + N_LORA_PAD
P_GATE = P_Q + D_CA
P_ALL = P_GATE + N_BRANCH * D_MODEL

VMEM_LIMIT = 56 * 1024 * 1024


def _mm_kernel(a_ref, b_ref, o_ref):
    o_ref[...] = jnp.dot(a_ref[...].astype(jnp.bfloat16), b_ref[...].astype(jnp.bfloat16),
                         preferred_element_type=jnp.float32).astype(o_ref.dtype)


def _matmul(a, b, tm=1024, tn=512, out_dtype=jnp.float32):
    M, K = a.shape
    N = b.shape[1]
    tm = min(tm, M)
    tn = min(tn, N)
    assert M % tm == 0 and N % tn == 0, (M, N, tm, tn)
    return pl.pallas_call(
        _mm_kernel,
        grid=(M // tm, N // tn),
        in_specs=[pl.BlockSpec((tm, K), lambda i, j: (i, 0)),
                  pl.BlockSpec((K, tn), lambda i, j: (0, j))],
        out_specs=pl.BlockSpec((tm, tn), lambda i, j: (i, j)),
        out_shape=jax.ShapeDtypeStruct((M, N), out_dtype),
        compiler_params=pltpu.CompilerParams(dimension_semantics=("arbitrary", "arbitrary"),
                                             vmem_limit_bytes=VMEM_LIMIT),
        name="mm",
    )(a, b)


def _mm3(x, w, **kw):
    B, L, K = x.shape
    return _matmul(x.reshape(B * L, K).astype(jnp.bfloat16), w, **kw).reshape(B, L, w.shape[1])


def _moe_kernel(x_ref, wg_ref, wu_ref, wd_ref, o_ref):
    @pl.when(pl.program_id(2) == 0)
    def _():
        o_ref[...] = jnp.zeros_like(o_ref)

    x = x_ref[0]
    hg = jnp.dot(x, wg_ref[0].astype(jnp.bfloat16), preferred_element_type=jnp.float32)
    hu = jnp.dot(x, wu_ref[0].astype(jnp.bfloat16), preferred_element_type=jnp.float32)
    h = (hg * jax.nn.sigmoid(hg) * hu).astype(jnp.bfloat16)
    o_ref[0] += jnp.dot(h, wd_ref[0].astype(jnp.bfloat16), preferred_element_type=jnp.float32)


def _moe_ffn(xe, w_gate, w_up, w_down, tm=1024, tf=256):
    E, C, D = xe.shape
    F = w_gate.shape[2]
    tm = min(tm, C)
    return pl.pallas_call(
        _moe_kernel,
        grid=(E, C // tm, F // tf),
        in_specs=[pl.BlockSpec((1, tm, D), lambda e, i, f: (e, i, 0)),
                  pl.BlockSpec((1, D, tf), lambda e, i, f: (e, 0, f)),
                  pl.BlockSpec((1, D, tf), lambda e, i, f: (e, 0, f)),
                  pl.BlockSpec((1, tf, D), lambda e, i, f: (e, f, 0))],
        out_specs=pl.BlockSpec((1, tm, D), lambda e, i, f: (e, i, 0)),
        out_shape=jax.ShapeDtypeStruct((E, C, D), jnp.float32),
        compiler_params=pltpu.CompilerParams(dimension_semantics=("arbitrary",) * 3, vmem_limit_bytes=VMEM_LIMIT),
        name="moe_ffn",
    )(xe, w_gate, w_up, w_down)


WKV_CHUNK = 64
HEAD = RW_HEAD
PAIR = 2 * HEAD


def _bdot(a, b):
    return jnp.dot(a.astype(jnp.bfloat16), b.astype(jnp.bfloat16), preferred_element_type=jnp.float32)


def _bdot_nt(a, b):
    return lax.dot_general(a.astype(jnp.bfloat16), b.astype(jnp.bfloat16), (((1,), (1,)), ((), ())),
                           preferred_element_type=jnp.float32)


def _bdot_tn(a, b):
    return lax.dot_general(a.astype(jnp.bfloat16), b.astype(jnp.bfloat16), (((0,), (0,)), ((), ())),
                           preferred_element_type=jnp.float32)


def _wkv_chunk(rs, lws, ks, vs, kks, kkas, hs, rev):
    C = WKV_CHUNK
    f32 = jnp.float32
    n = len(rs)
    ri = lax.broadcasted_iota(jnp.int32, (C, C), 0)
    ci = lax.broadcasted_iota(jnp.int32, (C, C), 1)
    dd = jnp.where(rev == 0, ri - ci, ci - ri)
    tri = (dd >= 0).astype(f32)
    lane = lax.broadcasted_iota(jnp.int32, (C, PAIR), 1)
    first = lane < HEAD
    r2 = lax.broadcasted_iota(jnp.int32, (2 * C, 2 * C), 0)
    c2 = lax.broadcasted_iota(jnp.int32, (2 * C, 2 * C), 1)
    d2 = jnp.where(rev == 0, r2 - c2, c2 - r2)
    strict = d2 > 0
    incl = d2 >= 0
    eye = (r2 == c2).astype(f32)
    ones = jnp.ones((PAIR, PAIR), f32)

    def stack(x):
        return jnp.concatenate([jnp.where(first, x, 0.0), jnp.where(first, 0.0, x)], axis=0)

    cums = [jnp.dot(tri, lw, preferred_element_type=f32, precision=lax.Precision.HIGHEST) for lw in lws]
    galls = [jnp.exp(jnp.sum(lw, axis=0, keepdims=True)) for lw in lws]
    gcols = [jnp.dot(eye * ga, ones, preferred_element_type=f32, precision=lax.Precision.HIGHEST) for ga in galls]
    btS, atS, ktS, rtS, vS = [], [], [], [], []
    for i in range(n):
        cum, lw = cums[i], lws[i]
        ginv = jnp.exp(-cum)
        btS.append(stack(kks[i] * jnp.exp(cum - lw)).astype(jnp.bfloat16))
        atS.append(stack(-(kkas[i] * ginv)))
        ktS.append(stack(ks[i] * ginv))
        rtS.append(stack(rs[i] * jnp.exp(cum)).astype(jnp.bfloat16))
        vS.append(stack(vs[i]).astype(jnp.bfloat16))
    atB = [x.astype(jnp.bfloat16) for x in atS]
    ktB = [x.astype(jnp.bfloat16) for x in ktS]
    mab = [jnp.where(strict, _bdot_nt(btS[i], atB[i]), 0.0) for i in range(n)]
    mbk = [jnp.where(strict, _bdot_nt(btS[i], ktB[i]), 0.0).astype(jnp.bfloat16) for i in range(n)]
    mra = [jnp.where(incl, _bdot_nt(rtS[i], atB[i]), 0.0).astype(jnp.bfloat16) for i in range(n)]
    mrk = [jnp.where(incl, _bdot_nt(rtS[i], ktB[i]), 0.0).astype(jnp.bfloat16) for i in range(n)]
    ts = [eye + m for m in mab]
    ps = mab
    for _ in range(5):
        ps = [_bdot(p, p) for p in ps]
        ts = [t + _bdot(t, p) for t, p in zip(ts, ps)]
    hB = [h.astype(jnp.bfloat16) for h in hs]
    wS = [_bdot(btS[i], hB[i]) + _bdot(mbk[i], vS[i]) for i in range(n)]
    uS = [_bdot(ts[i], wS[i]).astype(jnp.bfloat16) for i in range(n)]
    yS = [_bdot(rtS[i], hB[i]) + _bdot(mra[i], uS[i]) + _bdot(mrk[i], vS[i]) for i in range(n)]
    ys = [y[:C] + y[C:] for y in yS]
    hn = [hs[i] * gcols[i] + _bdot_tn(atS[i] * galls[i], uS[i]) + _bdot_tn(ktS[i] * galls[i], vS[i]) for i in range(n)]
    return ys, hn


def _wkv_kernel(r_ref, v_ref, kk_ref, lw_ref, k_ref, kka_ref, o_ref, h_ref, *, pairs):
    rev = pl.program_id(0)

    @pl.when(pl.program_id(3) == 0)
    def _():
        h_ref[...] = jnp.zeros_like(h_ref)

    sls = [slice(p * PAIR, (p + 1) * PAIR) for p in range(pairs)]
    ys, hn = _wkv_chunk([r_ref[0, :, sl] for sl in sls], [lw_ref[0, 0, :, sl] for sl in sls],
                        [k_ref[0, 0, :, sl] for sl in sls], [v_ref[0, :, sl] for sl in sls],
                        [kk_ref[0, :, sl] for sl in sls], [kka_ref[0, 0, :, sl] for sl in sls],
                        [h_ref[p] for p in range(pairs)], rev)
    for p in range(pairs):
        o_ref[0, 0, :, sls[p]] = ys[p]
        h_ref[p] = hn[p]


def _wkv7(r, v, kk, lw, kdir, kka, pairs=8):
    B, L, D = r.shape
    C = WKV_CHUNK
    nc = L // C
    W = pairs * PAIR
    assert L % C == 0 and D % W == 0, (L, D)
    tmap = lambda d, b, g, c: (b, c + d * (nc - 1 - 2 * c), g)
    dmap = lambda d, b, g, c: (d, b, c + d * (nc - 1 - 2 * c), g)
    shared = pl.BlockSpec((1, C, W), tmap)
    perdir = pl.BlockSpec((1, 1, C, W), dmap)
    return pl.pallas_call(
        functools.partial(_wkv_kernel, pairs=pairs),
        grid=(2, B, D // W, nc),
        in_specs=[shared, shared, shared, perdir, perdir, perdir],
        out_specs=perdir,
        out_shape=jax.ShapeDtypeStruct((2, B, L, D), jnp.float32),
        scratch_shapes=[pltpu.VMEM((pairs, PAIR, PAIR), jnp.float32)],
        compiler_params=pltpu.CompilerParams(dimension_semantics=("arbitrary",) * 4),
        name="wkv7",
    )(r, v, kk, lw, kdir, kka)


def _layer_norm(x, g, b):
    xf = x.astype(jnp.float32)
    mu = jnp.mean(xf, axis=-1, keepdims=True)
    var = jnp.mean(jnp.square(xf - mu), axis=-1, keepdims=True)
    return ((xf - mu) * lax.rsqrt(var + LN_EPS) * g.astype(jnp.float32) + b.astype(jnp.float32)).astype(x.dtype)


def _short_conv3(u, w):
    up = jnp.pad(u, ((0, 0), (1, 1), (0, 0)))
    return w[0] * up[:, :-2] + w[1] * up[:, 1:-1] + w[2] * up[:, 2:]


def _hyena_filters(L, w1, b1, w2, b2, w3, b3, w4, sin_freq):
    f32 = jnp.float32
    t = jnp.linspace(0.0, 1.0, L, dtype=f32)[:, None]
    ang = 2.0 * math.pi * jnp.arange(L, dtype=f32)[:, None] / L
    bands = jnp.linspace(1e-4, HY_BANDS - 1, HY_BANDS, dtype=f32)[None]
    z = jnp.concatenate([t, jnp.cos(bands * ang), -jnp.sin(bands * ang)], axis=-1)
    sf = sin_freq.astype(f32)
    h = jnp.sin(sf[0] * (z @ w1.astype(f32) + b1.astype(f32)))
    h = jnp.sin(sf[1] * (h @ w2.astype(f32) + b2.astype(f32)))
    h = jnp.sin(sf[2] * (h @ w3.astype(f32) + b3.astype(f32)))
    h = (h @ w4.astype(f32)).reshape(L, HY_ORDER, 2, D_HY)
    deltas = jnp.abs(jnp.linspace(math.log(HY_DECAY_TARGET) / HY_SLOW_DECAY_PCT,
                                  math.log(HY_DECAY_TARGET) / HY_FAST_DECAY_PCT, D_HY, dtype=f32))
    window = jnp.exp(-t * deltas[None])
    return h * window[:, None, None, :]


def _bidir_fftconv(z, h_fwd, h_bwd, dbias):
    L = z.shape[1]
    kern = jnp.concatenate([h_fwd, jnp.zeros_like(h_fwd[:1]), h_bwd[:0:-1]], axis=0)
    kf = jnp.fft.rfft(kern, axis=0)
    zf = jnp.fft.rfft(z, n=2 * L, axis=1)
    y = jnp.fft.irfft(zf * kf[None], n=2 * L, axis=1)[:, :L]
    return y + z * dbias


def _hyena(u, filt, dbias):
    v, x1, x2 = jnp.split(u.astype(jnp.float32), 3, axis=-1)
    db = dbias.astype(jnp.float32)
    z = x1 * _bidir_fftconv(v, filt[:, 0, 0], filt[:, 0, 1], db[0])
    return x2 * _bidir_fftconv(z, filt[:, 1, 0], filt[:, 1, 1], db[1])


def _heads(t):
    return t.reshape(t.shape[:-1] + (RW_HEADS, RW_HEAD))


def _rwkv7(u, lo, w0, w2, a0, a2, g2, k_k, k_a, r_k, ln_w, ln_b):
    B, L, _ = u.shape
    f32 = jnp.float32
    r, k, v = jnp.split(u.astype(f32), 3, axis=-1)
    lo = lo.astype(f32)
    nw, na = 2 * RW_DECAY_LORA, 2 * RW_AAA_LORA
    xw = lo[..., :nw].reshape(B, L, 2, RW_DECAY_LORA)
    xa = lo[..., nw:nw + na].reshape(B, L, 2, RW_AAA_LORA)
    xg = lo[..., nw + na:]
    logw = -jax.nn.softplus(-(w0.astype(f32)[:, None, None]
                              + jnp.einsum('bldr,drc->dblc', jnp.tanh(xw), w2.astype(f32)))) - 0.5
    lw = -jnp.exp(logw)
    a = jax.nn.sigmoid(a0.astype(f32)[:, None, None] + jnp.einsum('bldr,drc->dblc', xa, a2.astype(f32)))
    kk = _heads(k * k_k.astype(f32))
    kk = (kk / jnp.maximum(jnp.sqrt(jnp.sum(kk * kk, axis=-1, keepdims=True)), 1e-12)).reshape(B, L, D_RW)
    kdir = k[None] * (1.0 + (a - 1.0) * k_a.astype(f32))
    y2 = _wkv7(r, v, kk, lw, kdir, kk[None] * a)
    y = _heads(y2[0] + y2[1])
    mu = jnp.mean(y, axis=-1, keepdims=True)
    var = jnp.mean(jnp.square(y - mu), axis=-1, keepdims=True)
    y = ((y - mu) * lax.rsqrt(var + RW_GN_EPS)).reshape(B, L, D_RW) * ln_w.astype(f32) + ln_b.astype(f32)
    rh, vh = _heads(r), _heads(v)
    bonus = jnp.sum(rh[None] * _heads(kdir) * r_k.astype(f32)[:, None, None], axis=-1, keepdims=True) * vh[None]
    y = y + jnp.sum(bonus, axis=0).reshape(B, L, D_RW)
    g = _mm3(jax.nn.sigmoid(xg), g2)
    return y * g


def _memory_attention(q, mem, w_kv):
    B, L, _ = q.shape
    M = mem.shape[1]
    kv = _mm3(mem, w_kv)
    k = kv[..., :D_CA].reshape(B, M, CA_HEADS, CA_HEAD)
    v = kv[..., D_CA:].reshape(B, M, CA_HEADS, CA_HEAD)
    qh = q.reshape(B, L, CA_HEADS, CA_HEAD)
    s = jnp.einsum('blhd,bmhd->bhlm', qh, k).astype(jnp.float32) * (CA_HEAD ** -0.5)
    prob = jax.nn.softmax(s, axis=-1).astype(v.dtype)
    return jnp.einsum('bhlm,bmhd->blhd', prob, v).reshape(B, L, D_CA)


def _expert_choice_moe(x, w_router, w_gate, w_up, w_down):
    B, L, D = x.shape
    T = B * L
    cap = (EC_CAPACITY * T) // N_EXPERTS
    xt = x.reshape(T, D)
    aff = jax.nn.softmax((xt @ w_router).astype(jnp.float32), axis=-1)
    gval, idx = lax.top_k(aff.T, cap)
    xe = xt.astype(jnp.bfloat16)[idx]
    out = _moe_ffn(xe, w_gate, w_up, w_down) * gval[..., None]
    y = jnp.zeros_like(xt).at[idx.reshape(-1)].add(out.reshape(-1, D))
    return y.reshape(B, L, D)


def _layer(x, mem, p):
    B, L, _ = x.shape
    dt = x.dtype
    proj = _mm3(x, p['w_all'])
    ws = p['w_short']
    sc_hy = _short_conv3(proj[..., P_HY:P_RW], ws[:, OFF_HY:OFF_RW])
    sc_rw = _short_conv3(proj[..., P_RW:P_LORA], ws[:, OFF_RW:OFF_LORA])
    sc_lo = _short_conv3(proj[..., P_LORA:P_LORA + N_LORA], ws[:, OFF_LORA:OFF_Q])
    filt = _hyena_filters(L, p['hy_w1'], p['hy_b1'], p['hy_w2'], p['hy_b2'], p['hy_w3'], p['hy_b3'],
                          p['hy_w4'], p['hy_sin_freq'])
    hy = _mm3(_hyena(sc_hy, filt, p['hy_dbias']).astype(dt), p['hy_w_o'])
    rw = _mm3(_rwkv7(sc_rw, sc_lo, p['rw_w0'], p['rw_w2'], p['rw_a0'],
                     p['rw_a2'], p['rw_g2'], p['rw_k_k'], p['rw_k_a'], p['rw_r_k'], p['rw_ln_w'],
                     p['rw_ln_b']).astype(dt), p['rw_w_o'])
    ca = _mm3(_memory_attention(proj[..., P_Q:P_GATE], mem, p['ca_w_kv']), p['ca_w_o'])
    gates = jax.nn.sigmoid(proj[..., P_GATE:].astype(jnp.float32)).astype(dt).reshape(B, L, N_BRANCH, D_MODEL)
    merged = gates[:, :, 0] * hy + gates[:, :, 1] * rw + gates[:, :, 2] * ca
    x = _layer_norm(DN_ALPHA * x + _mm3(merged, p['w_out']), p['ln1_g'], p['ln1_b'])
    moe = _expert_choice_moe(x, p['moe_w_router'], p['moe_w_gate'], p['moe_w_up'], p['moe_w_down'])
    return _layer_norm(DN_ALPHA * x + moe, p['ln2_g'], p['ln2_b'])


def kernel(x_prompt, x_sample, mem_prompt, mem_sample, w_in, w_short, hy_w1, hy_b1, hy_w2, hy_b2, hy_w3, hy_b3, hy_w4, hy_sin_freq, hy_dbias, hy_w_o, rw_w0, rw_w2, rw_a0, rw_a2, rw_g2, rw_k_k, rw_k_a, rw_r_k, rw_ln_w, rw_ln_b, rw_w_o, ca_w_kv, ca_w_o, w_out, ln1_g, ln1_b, moe_w_router, moe_w_gate, moe_w_up, moe_w_down, ln2_g, ln2_b):
    params = dict(w_in=w_in, w_short=w_short, hy_w1=hy_w1, hy_b1=hy_b1, hy_w2=hy_w2, hy_b2=hy_b2,
                  hy_w3=hy_w3, hy_b3=hy_b3, hy_w4=hy_w4, hy_sin_freq=hy_sin_freq, hy_dbias=hy_dbias,
                  hy_w_o=hy_w_o, rw_w0=rw_w0, rw_w2=rw_w2, rw_a0=rw_a0, rw_a2=rw_a2, rw_g2=rw_g2,
                  rw_k_k=rw_k_k, rw_k_a=rw_k_a, rw_r_k=rw_r_k, rw_ln_w=rw_ln_w, rw_ln_b=rw_ln_b,
                  rw_w_o=rw_w_o, ca_w_kv=ca_w_kv, ca_w_o=ca_w_o, w_out=w_out, ln1_g=ln1_g, ln1_b=ln1_b,
                  moe_w_router=moe_w_router, moe_w_gate=moe_w_gate, moe_w_up=moe_w_up,
                  moe_w_down=moe_w_down, ln2_g=ln2_g, ln2_b=ln2_b)
    y_prompt, y_sample = x_prompt, x_sample
    for i in range(DEPTH):
        lp = {name: arr[i] for name, arr in params.items()}
        wi = lp['w_in']
        lp['w_all'] = jnp.concatenate(
            [wi[:, :OFF_Q], jnp.zeros((D_MODEL, N_LORA_PAD - N_LORA), wi.dtype), wi[:, OFF_Q:]],
            axis=1).astype(jnp.bfloat16)
        for name in ('hy_w_o', 'rw_w_o', 'ca_w_o', 'ca_w_kv', 'w_out', 'rw_g2'):
            lp[name] = lp[name].astype(jnp.bfloat16)
        y_prompt = _layer(y_prompt, mem_prompt, lp)
        y_sample = _layer(y_sample, mem_sample, lp)
    return (y_prompt, y_sample)
```

```python
import functools
import math

import jax
import jax.numpy as jnp
from jax import lax
from jax.experimental import pallas as pl
from jax.experimental.pallas import tpu as pltpu

D_MODEL = 2048
DEPTH = 1
D_HY = D_MODEL // 2
HY_ORDER = 2
HY_EMB = 33
HY_BANDS = (HY_EMB - 1) // 2
HY_FFN = 64
HY_FAST_DECAY_PCT = 0.3
HY_SLOW_DECAY_PCT = 1.5
HY_DECAY_TARGET = 1e-2
D_RW = D_MODEL // 2
RW_HEAD = 64
RW_HEADS = D_RW // RW_HEAD
RW_DECAY_LORA = 64
RW_AAA_LORA = 64
RW_GATE_LORA = 160
RW_GN_EPS = 64e-5
D_CA = D_MODEL // 2
CA_HEADS = 4
CA_HEAD = D_CA // CA_HEADS
N_BRANCH = 3
N_EXPERTS = 16
EC_CAPACITY = 2
D_EXPERT = 2 * D_MODEL
LN_EPS = 1e-5
DN_ALPHA = (2 * DEPTH) ** 0.25

OFF_HY = 0
OFF_RW = OFF_HY + 3 * D_HY
OFF_LORA = OFF_RW + 3 * D_RW
N_LORA = 2 * RW_DECAY_LORA + 2 * RW_AAA_LORA + RW_GATE_LORA
OFF_Q = OFF_LORA + N_LORA
OFF_GATE = OFF_Q + D_CA
N_IN = OFF_GATE + N_BRANCH * D_MODEL

LANES = 128
N_LORA_PAD = -(-N_LORA // LANES) * LANES
P_HY = 0
P_RW = P_HY + 3 * D_HY
P_LORA = P_RW + 3 * D_RW
P_Q = P_LORA + N_LORA_PAD
P_GATE = P_Q + D_CA
P_ALL = P_GATE + N_BRANCH * D_MODEL

VMEM_LIMIT = 56 * 1024 * 1024
SUBLANES = 8
CB = 512
N_CONV = P_Q
LORA_W = N_LORA_PAD


def _mm_kernel(a_ref, b_ref, o_ref):
    o_ref[...] = jnp.dot(a_ref[...].astype(jnp.bfloat16), b_ref[...].astype(jnp.bfloat16),
                         preferred_element_type=jnp.float32).astype(o_ref.dtype)


def _matmul(a, b, tm=1024, tn=512, out_dtype=jnp.float32):
    M, K = a.shape
    N = b.shape[1]
    tm = min(tm, M)
    tn = min(tn, N)
    assert M % tm == 0 and N % tn == 0, (M, N, tm, tn)
    return pl.pallas_call(
        _mm_kernel,
        grid=(M // tm, N // tn),
        in_specs=[pl.BlockSpec((tm, K), lambda i, j: (i, 0)),
                  pl.BlockSpec((K, tn), lambda i, j: (0, j))],
        out_specs=pl.BlockSpec((tm, tn), lambda i, j: (i, j)),
        out_shape=jax.ShapeDtypeStruct((M, N), out_dtype),
        compiler_params=pltpu.CompilerParams(dimension_semantics=("arbitrary", "arbitrary"),
                                             vmem_limit_bytes=VMEM_LIMIT),
        name="mm",
    )(a, b)


def _mm3(x, w, **kw):
    B, L, K = x.shape
    return _matmul(x.reshape(B * L, K).astype(jnp.bfloat16), w, **kw).reshape(B, L, w.shape[1])


def _moe_kernel(x_ref, wg_ref, wu_ref, wd_ref, o_ref):
    @pl.when(pl.program_id(2) == 0)
    def _():
        o_ref[...] = jnp.zeros_like(o_ref)

    x = x_ref[0]
    hg = jnp.dot(x, wg_ref[0].astype(jnp.bfloat16), preferred_element_type=jnp.float32)
    hu = jnp.dot(x, wu_ref[0].astype(jnp.bfloat16), preferred_element_type=jnp.float32)
    h = (hg * jax.nn.sigmoid(hg) * hu).astype(jnp.bfloat16)
    o_ref[0] += jnp.dot(h, wd_ref[0].astype(jnp.bfloat16), preferred_element_type=jnp.float32)


def _moe_ffn(xe, w_gate, w_up, w_down, tm=1024, tf=256):
    E, C, D = xe.shape
    F = w_gate.shape[2]
    tm = min(tm, C)
    return pl.pallas_call(
        _moe_kernel,
        grid=(E, C // tm, F // tf),
        in_specs=[pl.BlockSpec((1, tm, D), lambda e, i, f: (e, i, 0)),
                  pl.BlockSpec((1, D, tf), lambda e, i, f: (e, 0, f)),
                  pl.BlockSpec((1, D, tf), lambda e, i, f: (e, 0, f)),
                  pl.BlockSpec((1, tf, D), lambda e, i, f: (e, f, 0))],
        out_specs=pl.BlockSpec((1, tm, D), lambda e, i, f: (e, i, 0)),
        out_shape=jax.ShapeDtypeStruct((E, C, D), jnp.float32),
        compiler_params=pltpu.CompilerParams(dimension_semantics=("arbitrary",) * 3, vmem_limit_bytes=VMEM_LIMIT),
        name="moe_ffn",
    )(xe, w_gate, w_up, w_down)


WKV_CHUNK = 64
HEAD = RW_HEAD
PAIR = 2 * HEAD


def _bdot(a, b):
    return jnp.dot(a.astype(jnp.bfloat16), b.astype(jnp.bfloat16), preferred_element_type=jnp.float32)


def _bdot_nt(a, b):
    return lax.dot_general(a.astype(jnp.bfloat16), b.astype(jnp.bfloat16), (((1,), (1,)), ((), ())),
                           preferred_element_type=jnp.float32)


def _bdot_tn(a, b):
    return lax.dot_general(a.astype(jnp.bfloat16), b.astype(jnp.bfloat16), (((0,), (0,)), ((), ())),
                           preferred_element_type=jnp.float32)


def _wkv_chunk(rs, lws, ks, vs, kks, kkas, hs, rev):
    C = WKV_CHUNK
    f32 = jnp.float32
    n = len(rs)
    ri = lax.broadcasted_iota(jnp.int32, (C, C), 0)
    ci = lax.broadcasted_iota(jnp.int32, (C, C), 1)
    dd = jnp.where(rev == 0, ri - ci, ci - ri)
    tri = (dd >= 0).astype(f32)
    lane = lax.broadcasted_iota(jnp.int32, (C, PAIR), 1)
    first = lane < HEAD
    r2 = lax.broadcasted_iota(jnp.int32, (2 * C, 2 * C), 0)
    c2 = lax.broadcasted_iota(jnp.int32, (2 * C, 2 * C), 1)
    d2 = jnp.where(rev == 0, r2 - c2, c2 - r2)
    strict = d2 > 0
    incl = d2 >= 0
    eye = (r2 == c2).astype(f32)
    ones = jnp.ones((PAIR, PAIR), f32)

    def stack(x):
        return jnp.concatenate([jnp.where(first, x, 0.0), jnp.where(first, 0.0, x)], axis=0)

    cums = [jnp.dot(tri, lw, preferred_element_type=f32, precision=lax.Precision.HIGHEST) for lw in lws]
    galls = [jnp.exp(jnp.sum(lw, axis=0, keepdims=True)) for lw in lws]
    gcols = [jnp.dot(eye * ga, ones, preferred_element_type=f32, precision=lax.Precision.HIGHEST) for ga in galls]
    btS, atS, ktS, rtS, vS = [], [], [], [], []
    for i in range(n):
        cum, lw = cums[i], lws[i]
        ginv = jnp.exp(-cum)
        btS.append(stack(kks[i] * jnp.exp(cum - lw)).astype(jnp.bfloat16))
        atS.append(stack(-(kkas[i] * ginv)))
        ktS.append(stack(ks[i] * ginv))
        rtS.append(stack(rs[i] * jnp.exp(cum)).astype(jnp.bfloat16))
        vS.append(stack(vs[i]).astype(jnp.bfloat16))
    atB = [x.astype(jnp.bfloat16) for x in atS]
    ktB = [x.astype(jnp.bfloat16) for x in ktS]
    mab = [jnp.where(strict, _bdot_nt(btS[i], atB[i]), 0.0) for i in range(n)]
    mbk = [jnp.where(strict, _bdot_nt(btS[i], ktB[i]), 0.0).astype(jnp.bfloat16) for i in range(n)]
    mra = [jnp.where(incl, _bdot_nt(rtS[i], atB[i]), 0.0).astype(jnp.bfloat16) for i in range(n)]
    mrk = [jnp.where(incl, _bdot_nt(rtS[i], ktB[i]), 0.0).astype(jnp.bfloat16) for i in range(n)]
    ts = [eye + m for m in mab]
    ps = mab
    for _ in range(5):
        ps = [_bdot(p, p) for p in ps]
        ts = [t + _bdot(t, p) for t, p in zip(ts, ps)]
    hB = [h.astype(jnp.bfloat16) for h in hs]
    wS = [_bdot(btS[i], hB[i]) + _bdot(mbk[i], vS[i]) for i in range(n)]
    uS = [_bdot(ts[i], wS[i]).astype(jnp.bfloat16) for i in range(n)]
    yS = [_bdot(rtS[i], hB[i]) + _bdot(mra[i], uS[i]) + _bdot(mrk[i], vS[i]) for i in range(n)]
    ys = [y[:C] + y[C:] for y in yS]
    hn = [hs[i] * gcols[i] + _bdot_tn(atS[i] * galls[i], uS[i]) + _bdot_tn(ktS[i] * galls[i], vS[i]) for i in range(n)]
    return ys, hn


CONV_ROWS = 256


def _sconv_kernel(x_ref, w_ref, o_ref, *, L):
    TR = CONV_ROWS
    w0, w1, w2 = w_ref[0:1, :], w_ref[1:2, :], w_ref[2:3, :]
    rid = lax.broadcasted_iota(jnp.int32, (TR, 1), 0)

    def body(i, carry):
        r0 = pl.multiple_of(i * TR, TR)
        cur = x_ref[0, pl.ds(r0, TR), :]
        up = x_ref[0, pl.ds(pl.multiple_of(jnp.maximum(r0 - SUBLANES, 0), SUBLANES), SUBLANES), :][SUBLANES - 1:SUBLANES, :]
        dn = x_ref[0, pl.ds(pl.multiple_of(jnp.minimum(r0 + TR, L - SUBLANES), SUBLANES), SUBLANES), :][0:1, :]
        up = jnp.where(i == 0, 0.0, up)
        dn = jnp.where(r0 + TR >= L, 0.0, dn)
        prev = jnp.where(rid == 0, up, pltpu.roll(cur, 1, 0))
        nxt = jnp.where(rid == TR - 1, dn, pltpu.roll(cur, TR - 1, 0))
        o_ref[0, pl.ds(r0, TR), :] = w0 * prev + w1 * cur + w2 * nxt
        return carry

    lax.fori_loop(0, L // TR, body, 0)


def short_conv(proj, w, ncols, tw=256):
    B, L, _ = proj.shape
    assert ncols % tw == 0 and L % CONV_ROWS == 0
    return pl.pallas_call(
        functools.partial(_sconv_kernel, L=L),
        grid=(B, ncols // tw),
        in_specs=[pl.BlockSpec((1, L, tw), lambda b, c: (b, 0, c)),
                  pl.BlockSpec((3, tw), lambda b, c: (0, c))],
        out_specs=pl.BlockSpec((1, L, tw), lambda b, c: (b, 0, c)),
        out_shape=jax.ShapeDtypeStruct((B, L, ncols), jnp.float32),
        compiler_params=pltpu.CompilerParams(dimension_semantics=("arbitrary",) * 2, vmem_limit_bytes=VMEM_LIMIT),
        name="short_conv",
    )(proj, w)


def _dft_mats(L):
    N = 2 * L
    f = lax.broadcasted_iota(jnp.int32, (L, L), 0)
    s = lax.broadcasted_iota(jnp.int32, (L, L), 1)
    ang = ((f * s) % N).astype(jnp.float32) * (2.0 * math.pi / N)
    a = jnp.cos(ang)
    sn = jnp.sin(ang)
    alt_s = (1 - 2 * (s % 2)).astype(jnp.float32)
    alt_f = (1 - 2 * (f % 2)).astype(jnp.float32)
    b = jnp.where(f == 0, alt_s, sn)
    b2 = jnp.where(s == 0, alt_f, sn)
    return a.astype(jnp.bfloat16), b.astype(jnp.bfloat16), b2.astype(jnp.bfloat16)


def _hy_fwd_kernel(z_ref, a_ref, b_ref, kp_ref, kq_ref, yp_ref, yq_ref, zb_ref, *, tf):
    j = pl.program_id(2)

    @pl.when(j == 0)
    def _():
        zb_ref[...] = z_ref[0].astype(jnp.bfloat16)

    zb = zb_ref[...]
    p = jnp.dot(a_ref[...], zb, preferred_element_type=jnp.float32)
    q = jnp.dot(b_ref[...], zb, preferred_element_type=jnp.float32)
    kp = kp_ref[...]
    kq = kq_ref[...]
    row = lax.broadcasted_iota(jnp.int32, p.shape, 0) + j * tf
    packed = row == 0
    kqq = kq * q
    yp_ref[0] = (kp * p - jnp.where(packed, 0.0, kqq)).astype(yp_ref.dtype)
    yq_ref[0] = jnp.where(packed, kqq, kp * q + kq * p).astype(yq_ref.dtype)


def _hy_fwd(z, zcol, a, b, kp, kq, tf=512, tc=512):
    B, L, _ = z.shape
    C = kp.shape[1]
    tf = min(tf, L)
    zspec = pl.BlockSpec((1, L, tc), lambda b_, c, j: (b_, 0, zcol + c))
    mspec = pl.BlockSpec((tf, L), lambda b_, c, j: (j, 0))
    kspec = pl.BlockSpec((tf, tc), lambda b_, c, j: (j, c))
    ospec = pl.BlockSpec((1, tf, tc), lambda b_, c, j: (b_, j, c))
    return pl.pallas_call(
        functools.partial(_hy_fwd_kernel, tf=tf),
        grid=(B, C // tc, L // tf),
        in_specs=[zspec, mspec, mspec, kspec, kspec],
        out_specs=[ospec, ospec],
        out_shape=[jax.ShapeDtypeStruct((B, L, C), jnp.bfloat16)] * 2,
        scratch_shapes=[pltpu.VMEM((L, tc), jnp.bfloat16)],
        compiler_params=pltpu.CompilerParams(dimension_semantics=("arbitrary",) * 3, vmem_limit_bytes=VMEM_LIMIT),
        name="hy_fwd",
    )(z, a, b, kp, kq)


def _hy_inv_kernel(yp_ref, yq_ref, a_ref, b2_ref, z_ref, x_ref, db_ref, o_ref):
    y = jnp.dot(a_ref[...], yp_ref[0], preferred_element_type=jnp.float32)
    y += jnp.dot(b2_ref[...], yq_ref[0], preferred_element_type=jnp.float32)
    o_ref[0] = (x_ref[0] * (y + z_ref[0] * db_ref[...])).astype(o_ref.dtype)


def _hy_inv(yp, yq, a, b2, zarr, zcol, xarr, xcol, db, out_dtype, tt=512, tc=512):
    B, L, C = yp.shape
    tt = min(tt, L)
    yspec = pl.BlockSpec((1, L, tc), lambda b_, c, i: (b_, 0, c))
    mspec = pl.BlockSpec((tt, L), lambda b_, c, i: (i, 0))
    return pl.pallas_call(
        _hy_inv_kernel,
        grid=(B, C // tc, L // tt),
        in_specs=[yspec, yspec, mspec, mspec,
                  pl.BlockSpec((1, tt, tc), lambda b_, c, i: (b_, i, zcol + c)),
                  pl.BlockSpec((1, tt, tc), lambda b_, c, i: (b_, i, xcol + c)),
                  pl.BlockSpec((1, tc), lambda b_, c, i: (0, c))],
        out_specs=pl.BlockSpec((1, tt, tc), lambda b_, c, i: (b_, i, c)),
        out_shape=jax.ShapeDtypeStruct((B, L, C), out_dtype),
        compiler_params=pltpu.CompilerParams(dimension_semantics=("arbitrary",) * 3, vmem_limit_bytes=VMEM_LIMIT),
        name="hy_inv",
    )(yp, yq, a, b2, zarr, xarr, db)


def _hy_spectra(filt, a, b, mm):
    L = filt.shape[0]
    N = 2 * L
    row0 = (jnp.arange(L) == 0)[:, None]
    hs = []
    for o in range(filt.shape[1]):
        hs += [filt[:, o, 0], jnp.where(row0, 0.0, filt[:, o, 1])]
    hcat = jnp.concatenate(hs, axis=1).astype(jnp.bfloat16)
    cp = mm(a, hcat)
    sp = mm(b, hcat)
    C = filt.shape[3]
    wgt = jnp.where(row0, 1.0 / N, 2.0 / N)
    out = []
    for o in range(filt.shape[1]):
        cf, cb = cp[:, (2 * o) * C:(2 * o + 1) * C], cp[:, (2 * o + 1) * C:(2 * o + 2) * C]
        sf, sb = sp[:, (2 * o) * C:(2 * o + 1) * C], sp[:, (2 * o + 1) * C:(2 * o + 2) * C]
        out.append(((cf + cb) * wgt, jnp.where(row0, sf + sb, sf - sb) * wgt))
    return out


def hyena(sc, filt, dbias, mm, tc=512):
    L = sc.shape[1]
    C = filt.shape[3]
    a, b, b2 = _dft_mats(L)
    (kp1, kq1), (kp2, kq2) = _hy_spectra(filt, a, b, mm)
    db = dbias.astype(jnp.float32)
    nb = C // tc
    yp, yq = _hy_fwd(sc, 0, a, b, kp1, kq1, tc=tc)
    z2 = _hy_inv(yp, yq, a, b2, sc, 0, sc, nb, db[0:1], jnp.float32, tc=tc)
    yp, yq = _hy_fwd(z2, 0, a, b, kp2, kq2, tc=tc)
    return _hy_inv(yp, yq, a, b2, z2, 0, sc, 2 * nb, db[1:2], jnp.bfloat16, tc=tc)


def _split_dot(x, ones_bf16):
    hi = x.astype(jnp.bfloat16)
    lo = (x - hi.astype(jnp.float32)).astype(jnp.bfloat16)
    return (jnp.dot(hi, ones_bf16, preferred_element_type=jnp.float32)
            + jnp.dot(lo, ones_bf16, preferred_element_type=jnp.float32))


def _head_ones():
    r = lax.broadcasted_iota(jnp.int32, (PAIR, PAIR), 0) // HEAD
    c = lax.broadcasted_iota(jnp.int32, (PAIR, PAIR), 1) // HEAD
    return (r == c).astype(jnp.bfloat16)


def _group_sum(x, ones):
    return jnp.concatenate([_split_dot(x[:, p * PAIR:(p + 1) * PAIR], ones) for p in range(x.shape[1] // PAIR)], axis=1)


def _softplus(x):
    return jnp.maximum(x, 0.0) + jnp.log1p(jnp.exp(-jnp.abs(x)))


def _wkv_fused_kernel(r_ref, k_ref, v_ref, lo_ref, w0_ref, a0_ref, w2_ref, a2_ref, kk_ref, ka_ref, o_ref, h_ref):
    rev = pl.program_id(0)

    @pl.when(pl.program_id(2) == 0)
    def _():
        h_ref[...] = jnp.zeros_like(h_ref)

    f32 = jnp.float32
    r = r_ref[0]
    k = k_ref[0]
    v = v_ref[0]
    xw = lo_ref[0, :, 0:PAIR]
    xa = lo_ref[0, :, PAIR:2 * PAIR]
    dw = jnp.dot(jnp.tanh(xw).astype(jnp.bfloat16), w2_ref[0], preferred_element_type=f32)
    lw = -jnp.exp(-_softplus(-(w0_ref[0] + dw)) - 0.5)
    a = jax.nn.sigmoid(a0_ref[0] + jnp.dot(xa.astype(jnp.bfloat16), a2_ref[0], preferred_element_type=f32))
    kq = k * kk_ref[...]
    ss = _group_sum(kq * kq, _head_ones())
    kk = kq / jnp.maximum(jnp.sqrt(ss), 1e-12)
    kdir = k * (1.0 + (a - 1.0) * ka_ref[...])
    kka = kk * a
    n = D_RW // PAIR
    sls = [slice(p * PAIR, (p + 1) * PAIR) for p in range(n)]
    ys, hn = _wkv_chunk([r[:, s] for s in sls], [lw[:, s] for s in sls], [kdir[:, s] for s in sls],
                        [v[:, s] for s in sls], [kk[:, s] for s in sls], [kka[:, s] for s in sls],
                        [h_ref[p] for p in range(n)], rev)
    for p in range(n):
        o_ref[0, 0, :, sls[p]] = ys[p]
        h_ref[p] = hn[p]


def wkv7_fused(sc, rcol, lcol, w0, a0, w2p, a2p, k_k, k_a):
    B, L, _ = sc.shape
    C = WKV_CHUNK
    nc = L // C
    D = D_RW
    assert L % C == 0
    cm = lambda d, c: c + d * (nc - 1 - 2 * c)
    col = lambda j: pl.BlockSpec((1, C, D), lambda d, b, c: (b, cm(d, c), j))
    vec2 = pl.BlockSpec((1, 1, D), lambda d, b, c: (d, 0, 0))
    mat2 = pl.BlockSpec((1, PAIR, D), lambda d, b, c: (d, 0, 0))
    vec = pl.BlockSpec((1, D), lambda d, b, c: (0, 0))
    return pl.pallas_call(
        _wkv_fused_kernel,
        grid=(2, B, nc),
        in_specs=[col(rcol), col(rcol + 1), col(rcol + 2),
                  pl.BlockSpec((1, C, LORA_W), lambda d, b, c: (b, cm(d, c), lcol)),
                  vec2, vec2, mat2, mat2, vec, vec],
        out_specs=pl.BlockSpec((1, 1, C, D), lambda d, b, c: (d, b, cm(d, c), 0)),
        out_shape=jax.ShapeDtypeStruct((2, B, L, D), jnp.float32),
        scratch_shapes=[pltpu.VMEM((D // PAIR, PAIR, PAIR), jnp.float32)],
        compiler_params=pltpu.CompilerParams(dimension_semantics=("arbitrary",) * 3, vmem_limit_bytes=VMEM_LIMIT),
        name="wkv7",
    )(sc, sc, sc, sc, w0, a0, w2p, a2p, k_k, k_a)


def _rw_post_kernel(y_ref, r_ref, k_ref, v_ref, lo_ref, a0_ref, a2_ref, ka_ref, rk_ref, lnw_ref, lnb_ref, g2_ref, wo_ref, o_ref):
    f32 = jnp.float32
    ones = _head_ones()
    y = y_ref[0, 0] + y_ref[1, 0]
    mu = _group_sum(y, ones) * (1.0 / HEAD)
    yc = y - mu
    var = _group_sum(yc * yc, ones) * (1.0 / HEAD)
    out = yc * lax.rsqrt(var + RW_GN_EPS) * lnw_ref[...] + lnb_ref[...]
    r = r_ref[0]
    k = k_ref[0]
    v = v_ref[0]
    xa = lo_ref[0, :, PAIR:2 * PAIR].astype(jnp.bfloat16)
    for d in range(2):
        a = jax.nn.sigmoid(a0_ref[d] + jnp.dot(xa, a2_ref[d], preferred_element_type=f32))
        kdir = k * (1.0 + (a - 1.0) * ka_ref[...])
        out = out + _group_sum(r * kdir * rk_ref[d], ones) * v
    g = jnp.dot(jax.nn.sigmoid(lo_ref[0]).astype(jnp.bfloat16), g2_ref[...], preferred_element_type=f32)
    o_ref[0] = jnp.dot((out * g).astype(jnp.bfloat16), wo_ref[...], preferred_element_type=f32)


def rw_post(y2, sc, rcol, lcol, a0, a2p, k_a, r_k, ln_w, ln_b, g2p, w_o, tm=256):
    _, B, L, D = y2.shape
    DM = w_o.shape[1]
    tm = min(tm, L)
    col = lambda j: pl.BlockSpec((1, tm, D), lambda b, i: (b, i, j))
    full = lambda shape: pl.BlockSpec(shape, lambda b, i: (0,) * len(shape))
    return pl.pallas_call(
        _rw_post_kernel,
        grid=(B, L // tm),
        in_specs=[pl.BlockSpec((2, 1, tm, D), lambda b, i: (0, b, i, 0)),
                  col(rcol), col(rcol + 1), col(rcol + 2),
                  pl.BlockSpec((1, tm, LORA_W), lambda b, i: (b, i, lcol)),
                  full((2, 1, D)), full((2, PAIR, D)), full((1, D)), full((2, 1, D)), full((1, D)), full((1, D)),
                  full((LORA_W, D)), full((D, DM))],
        out_specs=pl.BlockSpec((1, tm, DM), lambda b, i: (b, i, 0)),
        out_shape=jax.ShapeDtypeStruct((B, L, DM), jnp.float32),
        compiler_params=pltpu.CompilerParams(dimension_semantics=("arbitrary",) * 2, vmem_limit_bytes=VMEM_LIMIT),
        name="rw_post",
    )(y2, sc, sc, sc, sc, a0, a2p, k_a, r_k, ln_w, ln_b, g2p, w_o)


def rwkv_params(p):
    f32, bf = jnp.float32, jnp.bfloat16
    z64 = jnp.zeros((64, D_RW), f32)
    w2p = jnp.stack([jnp.concatenate([p['rw_w2'][0], z64]), jnp.concatenate([z64, p['rw_w2'][1]])]).astype(bf)
    a2p = jnp.stack([jnp.concatenate([p['rw_a2'][0], z64]), jnp.concatenate([z64, p['rw_a2'][1]])]).astype(bf)
    g2p = jnp.zeros((LORA_W, D_RW), f32).at[256:256 + 160].set(p['rw_g2']).astype(bf)
    return dict(w0=p['rw_w0'].reshape(2, 1, D_RW), a0=p['rw_a0'].reshape(2, 1, D_RW), w2p=w2p, a2p=a2p, g2p=g2p,
                k_k=p['rw_k_k'].reshape(1, D_RW), k_a=p['rw_k_a'].reshape(1, D_RW), r_k=p['rw_r_k'].reshape(2, 1, D_RW),
                ln_w=p['rw_ln_w'].reshape(1, D_RW), ln_b=p['rw_ln_b'].reshape(1, D_RW), w_o=p['rw_w_o'].astype(bf))


def rwkv(sc, rcol, lcol, q):
    y2 = wkv7_fused(sc, rcol, lcol, q['w0'], q['a0'], q['w2p'], q['a2p'], q['k_k'], q['k_a'])
    return rw_post(y2, sc, rcol, lcol, q['a0'], q['a2p'], q['k_a'], q['r_k'], q['ln_w'], q['ln_b'], q['g2p'], q['w_o'])


def _attn_kernel(q_ref, k_ref, v_ref, wo_ref, o_ref):
    hp = pl.program_id(2)

    @pl.when(hp == 0)
    def _():
        o_ref[...] = jnp.zeros_like(o_ref)

    outs = []
    for h in range(CB // CA_HEAD):
        sl = slice(h * CA_HEAD, (h + 1) * CA_HEAD)
        q = q_ref[0, :, sl].astype(jnp.bfloat16)
        s = lax.dot_general(q, k_ref[0, :, sl].astype(jnp.bfloat16), (((1,), (1,)), ((), ())),
                            preferred_element_type=jnp.float32) * (CA_HEAD ** -0.5)
        s = s - jnp.max(s, axis=-1, keepdims=True)
        e = jnp.exp(s)
        p = e / jnp.sum(e, axis=-1, keepdims=True)
        outs.append(jnp.dot(p.astype(jnp.bfloat16), v_ref[0, :, sl].astype(jnp.bfloat16), preferred_element_type=jnp.float32))
    att = jnp.concatenate(outs, axis=1).astype(jnp.bfloat16)
    o_ref[0] += jnp.dot(att, wo_ref[...], preferred_element_type=jnp.float32)


def mem_attention(proj, qcol, kv, w_o, tm=512):
    B, L, _ = proj.shape
    M = kv.shape[1]
    DM = w_o.shape[1]
    nhp = w_o.shape[0] // CB
    tm = min(tm, L)
    return pl.pallas_call(
        _attn_kernel,
        grid=(B, L // tm, nhp),
        in_specs=[pl.BlockSpec((1, tm, CB), lambda b, i, h: (b, i, qcol + h)),
                  pl.BlockSpec((1, M, CB), lambda b, i, h: (b, 0, h)),
                  pl.BlockSpec((1, M, CB), lambda b, i, h: (b, 0, nhp + h)),
                  pl.BlockSpec((CB, DM), lambda b, i, h: (h, 0))],
        out_specs=pl.BlockSpec((1, tm, DM), lambda b, i, h: (b, i, 0)),
        out_shape=jax.ShapeDtypeStruct((B, L, DM), jnp.float32),
        compiler_params=pltpu.CompilerParams(dimension_semantics=("arbitrary",) * 3, vmem_limit_bytes=VMEM_LIMIT),
        name="mem_attn",
    )(proj, kv, kv, w_o)


def _merge_kernel(hy_ref, rw_ref, ca_ref, g0_ref, g1_ref, g2_ref, wo_ref, x_ref, lg_ref, lb_ref, wr_ref,
                  o_ref, ob_ref, lo_ref, acc_ref):
    kc = pl.program_id(2)

    @pl.when(kc == 0)
    def _():
        acc_ref[...] = jnp.zeros_like(acc_ref)

    m = (jax.nn.sigmoid(g0_ref[0]) * hy_ref[0] + jax.nn.sigmoid(g1_ref[0]) * rw_ref[0]
         + jax.nn.sigmoid(g2_ref[0]) * ca_ref[0])
    acc_ref[...] += jnp.dot(m.astype(jnp.bfloat16), wo_ref[...], preferred_element_type=jnp.float32)

    @pl.when(kc == pl.num_programs(2) - 1)
    def _():
        h = DN_ALPHA * x_ref[0] + acc_ref[...]
        mu = jnp.mean(h, axis=-1, keepdims=True)
        hc = h - mu
        var = jnp.mean(hc * hc, axis=-1, keepdims=True)
        y = hc * lax.rsqrt(var + LN_EPS) * lg_ref[...] + lb_ref[...]
        o_ref[0] = y
        yb = y.astype(jnp.bfloat16)
        ob_ref[0] = yb
        lo_ref[0] = jnp.dot(yb, wr_ref[...], preferred_element_type=jnp.float32)


def merge_ln(hy, rw, ca, proj, gcol, w_out, x, ln_g, ln_b, w_router_pad, tm=512):
    B, L, DM = x.shape
    nk = DM // CB
    tm = min(tm, L)
    br = pl.BlockSpec((1, tm, CB), lambda b, i, k: (b, i, k))
    gate = lambda g: pl.BlockSpec((1, tm, CB), lambda b, i, k: (b, i, gcol + g * nk + k))
    row = pl.BlockSpec((1, tm, DM), lambda b, i, k: (b, i, 0))
    vec = pl.BlockSpec((1, DM), lambda b, i, k: (0, 0))
    return pl.pallas_call(
        _merge_kernel,
        grid=(B, L // tm, nk),
        in_specs=[br, br, br, gate(0), gate(1), gate(2),
                  pl.BlockSpec((CB, DM), lambda b, i, k: (k, 0)), row, vec, vec,
                  pl.BlockSpec((DM, LANES), lambda b, i, k: (0, 0))],
        out_specs=[row, row, pl.BlockSpec((1, tm, LANES), lambda b, i, k: (b, i, 0))],
        out_shape=[jax.ShapeDtypeStruct((B, L, DM), jnp.float32), jax.ShapeDtypeStruct((B, L, DM), jnp.bfloat16),
                   jax.ShapeDtypeStruct((B, L, LANES), jnp.float32)],
        scratch_shapes=[pltpu.VMEM((tm, DM), jnp.float32)],
        compiler_params=pltpu.CompilerParams(dimension_semantics=("arbitrary",) * 3, vmem_limit_bytes=VMEM_LIMIT),
        name="merge_ln",
    )(hy, rw, ca, proj, proj, proj, w_out, x, ln_g, ln_b, w_router_pad)


def _ln_kernel(x_ref, m_ref, g_ref, b_ref, o_ref):
    h = DN_ALPHA * x_ref[...] + m_ref[...]
    mu = jnp.mean(h, axis=-1, keepdims=True)
    hc = h - mu
    var = jnp.mean(hc * hc, axis=-1, keepdims=True)
    o_ref[...] = hc * lax.rsqrt(var + LN_EPS) * g_ref[...] + b_ref[...]


def resid_ln(x, m, g, b, tm=512):
    T, DM = x.shape
    row = pl.BlockSpec((tm, DM), lambda i: (i, 0))
    vec = pl.BlockSpec((1, DM), lambda i: (0, 0))
    return pl.pallas_call(
        _ln_kernel, grid=(T // tm,), in_specs=[row, row, vec, vec], out_specs=row,
        out_shape=jax.ShapeDtypeStruct((T, DM), jnp.float32),
        compiler_params=pltpu.CompilerParams(dimension_semantics=("arbitrary",), vmem_limit_bytes=VMEM_LIMIT),
        name="resid_ln",
    )(x, m, g, b)


def _hyena_filters(L, w1, b1, w2, b2, w3, b3, w4, sin_freq):
    f32 = jnp.float32
    t = jnp.linspace(0.0, 1.0, L, dtype=f32)[:, None]
    ang = 2.0 * math.pi * jnp.arange(L, dtype=f32)[:, None] / L
    bands = jnp.linspace(1e-4, HY_BANDS - 1, HY_BANDS, dtype=f32)[None]
    z = jnp.concatenate([t, jnp.cos(bands * ang), -jnp.sin(bands * ang)], axis=-1)
    sf = sin_freq.astype(f32)
    h = jnp.sin(sf[0] * (z @ w1.astype(f32) + b1.astype(f32)))
    h = jnp.sin(sf[1] * (h @ w2.astype(f32) + b2.astype(f32)))
    h = jnp.sin(sf[2] * (h @ w3.astype(f32) + b3.astype(f32)))
    h = (h @ w4.astype(f32)).reshape(L, HY_ORDER, 2, D_HY)
    deltas = jnp.abs(jnp.linspace(math.log(HY_DECAY_TARGET) / HY_SLOW_DECAY_PCT,
                                  math.log(HY_DECAY_TARGET) / HY_FAST_DECAY_PCT, D_HY, dtype=f32))
    window = jnp.exp(-t * deltas[None])
    return h * window[:, None, None, :]


def _expert_choice_moe(xt, xb, logits, w_gate, w_up, w_down):
    T, D = xt.shape
    cap = (EC_CAPACITY * T) // N_EXPERTS
    aff = jax.nn.softmax(logits.astype(jnp.float32), axis=-1)
    gval, idx = lax.top_k(aff.T, cap)
    out = _moe_ffn(xb[idx], w_gate, w_up, w_down) * gval[..., None]
    return jnp.zeros_like(xt).at[idx.reshape(-1)].add(out.reshape(-1, D))


def _layer(x, mem, p):
    B, L, D = x.shape
    proj = _mm3(x, p['w_all'])
    sc = short_conv(proj, p['w_short_all'], N_CONV)
    filt = _hyena_filters(L, p['hy_w1'], p['hy_b1'], p['hy_w2'], p['hy_b2'], p['hy_w3'], p['hy_b3'],
                          p['hy_w4'], p['hy_sin_freq'])
    hy = _mm3(hyena(sc, filt, p['hy_dbias'], _matmul), p['hy_w_o'])
    rw = rwkv(sc, P_RW // D_RW, P_LORA // LORA_W, p['rw'])
    ca = mem_attention(proj, P_Q // CB, _mm3(mem, p['ca_w_kv']), p['ca_w_o'])
    x1, x1b, logits = merge_ln(hy, rw, ca, proj, P_GATE // CB, p['w_out'], x, p['ln1_g'], p['ln1_b'], p['w_router_pad'])
    moe = _expert_choice_moe(x1.reshape(B * L, D), x1b.reshape(B * L, D), logits.reshape(B * L, LANES)[:, :N_EXPERTS],
                             p['moe_w_gate'], p['moe_w_up'], p['moe_w_down'])
    return resid_ln(x1.reshape(B * L, D), moe, p['ln2_g'], p['ln2_b']).reshape(B, L, D)


def kernel(x_prompt, x_sample, mem_prompt, mem_sample, w_in, w_short, hy_w1, hy_b1, hy_w2, hy_b2, hy_w3, hy_b3, hy_w4, hy_sin_freq, hy_dbias, hy_w_o, rw_w0, rw_w2, rw_a0, rw_a2, rw_g2, rw_k_k, rw_k_a, rw_r_k, rw_ln_w, rw_ln_b, rw_w_o, ca_w_kv, ca_w_o, w_out, ln1_g, ln1_b, moe_w_router, moe_w_gate, moe_w_up, moe_w_down, ln2_g, ln2_b):
    params = dict(w_in=w_in, w_short=w_short, hy_w1=hy_w1, hy_b1=hy_b1, hy_w2=hy_w2, hy_b2=hy_b2,
                  hy_w3=hy_w3, hy_b3=hy_b3, hy_w4=hy_w4, hy_sin_freq=hy_sin_freq, hy_dbias=hy_dbias,
                  hy_w_o=hy_w_o, rw_w0=rw_w0, rw_w2=rw_w2, rw_a0=rw_a0, rw_a2=rw_a2, rw_g2=rw_g2,
                  rw_k_k=rw_k_k, rw_k_a=rw_k_a, rw_r_k=rw_r_k, rw_ln_w=rw_ln_w, rw_ln_b=rw_ln_b,
                  rw_w_o=rw_w_o, ca_w_kv=ca_w_kv, ca_w_o=ca_w_o, w_out=w_out, ln1_g=ln1_g, ln1_b=ln1_b,
                  moe_w_router=moe_w_router, moe_w_gate=moe_w_gate, moe_w_up=moe_w_up,
                  moe_w_down=moe_w_down, ln2_g=ln2_g, ln2_b=ln2_b)
    bf = jnp.bfloat16
    y_prompt, y_sample = x_prompt, x_sample
    for i in range(DEPTH):
        lp = {name: arr[i] for name, arr in params.items()}
        wi = lp['w_in']
        lp['w_all'] = jnp.concatenate(
            [wi[:, :OFF_Q], jnp.zeros((D_MODEL, N_LORA_PAD - N_LORA), wi.dtype), wi[:, OFF_Q:]], axis=1).astype(bf)
        lp['w_short_all'] = jnp.pad(lp['w_short'], ((0, 0), (0, N_LORA_PAD - N_LORA)))
        for name in ('hy_w_o', 'ca_w_o', 'ca_w_kv', 'w_out'):
            lp[name] = lp[name].astype(bf)
        lp['rw'] = rwkv_params(lp)
        lp['w_router_pad'] = jnp.pad(lp['moe_w_router'], ((0, 0), (0, LANES - N_EXPERTS))).astype(bf)
        for name in ('ln1_g', 'ln1_b', 'ln2_g', 'ln2_b'):
            lp[name] = lp[name].reshape(1, D_MODEL)
        y_prompt = _layer(y_prompt, mem_prompt, lp)
        y_sample = _layer(y_sample, mem_sample, lp)
    return (y_prompt, y_sample)
```

```python
import functools
import math

import jax
import jax.numpy as jnp
from jax import lax
from jax.experimental import pallas as pl
from jax.experimental.pallas import tpu as pltpu

D_MODEL = 2048
DEPTH = 1
D_HY = D_MODEL // 2
HY_ORDER = 2
HY_EMB = 33
HY_BANDS = (HY_EMB - 1) // 2
HY_FFN = 64
HY_FAST_DECAY_PCT = 0.3
HY_SLOW_DECAY_PCT = 1.5
HY_DECAY_TARGET = 1e-2
D_RW = D_MODEL // 2
RW_HEAD = 64
RW_HEADS = D_RW // RW_HEAD
RW_DECAY_LORA = 64
RW_AAA_LORA = 64
RW_GATE_LORA = 160
RW_GN_EPS = 64e-5
D_CA = D_MODEL // 2
CA_HEADS = 4
CA_HEAD = D_CA // CA_HEADS
N_BRANCH = 3
N_EXPERTS = 16
EC_CAPACITY = 2
D_EXPERT = 2 * D_MODEL
LN_EPS = 1e-5
DN_ALPHA = (2 * DEPTH) ** 0.25

OFF_HY = 0
OFF_RW = OFF_HY + 3 * D_HY
OFF_LORA = OFF_RW + 3 * D_RW
N_LORA = 2 * RW_DECAY_LORA + 2 * RW_AAA_LORA + RW_GATE_LORA
OFF_Q = OFF_LORA + N_LORA
OFF_GATE = OFF_Q + D_CA
N_IN = OFF_GATE + N_BRANCH * D_MODEL

LANES = 128
N_LORA_PAD = -(-N_LORA // LANES) * LANES
P_HY = 0
P_RW = P_HY + 3 * D_HY
P_LORA = P_RW + 3 * D_RW
P_Q = P_LORA + N_LORA_PAD
P_GATE = P_Q + D_CA
P_ALL = P_GATE + N_BRANCH * D_MODEL

VMEM_LIMIT = 56 * 1024 * 1024
SUBLANES = 8
CB = 512
N_CONV = P_Q
LORA_W = N_LORA_PAD


def _mm_kernel(a_ref, b_ref, o_ref):
    o_ref[...] = jnp.dot(a_ref[...].astype(jnp.bfloat16), b_ref[...].astype(jnp.bfloat16),
                         preferred_element_type=jnp.float32).astype(o_ref.dtype)


def _matmul(a, b, tm=1024, tn=512, out_dtype=jnp.float32):
    M, K = a.shape
    N = b.shape[1]
    tm = min(tm, M)
    tn = min(tn, N)
    assert M % tm == 0 and N % tn == 0, (M, N, tm, tn)
    return pl.pallas_call(
        _mm_kernel,
        grid=(M // tm, N // tn),
        in_specs=[pl.BlockSpec((tm, K), lambda i, j: (i, 0)),
                  pl.BlockSpec((K, tn), lambda i, j: (0, j))],
        out_specs=pl.BlockSpec((tm, tn), lambda i, j: (i, j)),
        out_shape=jax.ShapeDtypeStruct((M, N), out_dtype),
        compiler_params=pltpu.CompilerParams(dimension_semantics=("arbitrary", "arbitrary"),
                                             vmem_limit_bytes=VMEM_LIMIT),
        name="mm",
    )(a, b)


def _mm3(x, w, **kw):
    B, L, K = x.shape
    return _matmul(x.reshape(B * L, K), w, **kw).reshape(B, L, w.shape[1])


def _moe_kernel(x_ref, gv_ref, wg_ref, wu_ref, wd_ref, o_ref):
    @pl.when(pl.program_id(2) == 0)
    def _():
        o_ref[...] = jnp.zeros_like(o_ref)

    x = x_ref[0]
    hg = jnp.dot(x, wg_ref[0].astype(jnp.bfloat16), preferred_element_type=jnp.float32)
    hu = jnp.dot(x, wu_ref[0].astype(jnp.bfloat16), preferred_element_type=jnp.float32)
    h = (hg * jax.nn.sigmoid(hg) * hu).astype(jnp.bfloat16)
    o_ref[0] += jnp.dot(h, wd_ref[0].astype(jnp.bfloat16), preferred_element_type=jnp.float32)

    @pl.when(pl.program_id(2) == pl.num_programs(2) - 1)
    def _():
        o_ref[0] = o_ref[0] * jnp.tile(gv_ref[0], (1, o_ref.shape[2] // LANES))


def _moe_ffn(xe, gv, w_gate, w_up, w_down, tm=1024, tf=256):
    E, C, D = xe.shape
    F = w_gate.shape[2]
    tm = min(tm, C)
    return pl.pallas_call(
        _moe_kernel,
        grid=(E, C // tm, F // tf),
        in_specs=[pl.BlockSpec((1, tm, D), lambda e, i, f: (e, i, 0)),
                  pl.BlockSpec((1, tm, LANES), lambda e, i, f: (e, i, 0)),
                  pl.BlockSpec((1, D, tf), lambda e, i, f: (e, 0, f)),
                  pl.BlockSpec((1, D, tf), lambda e, i, f: (e, 0, f)),
                  pl.BlockSpec((1, tf, D), lambda e, i, f: (e, f, 0))],
        out_specs=pl.BlockSpec((1, tm, D), lambda e, i, f: (e, i, 0)),
        out_shape=jax.ShapeDtypeStruct((E, C, D), jnp.float32),
        compiler_params=pltpu.CompilerParams(dimension_semantics=("arbitrary",) * 3, vmem_limit_bytes=VMEM_LIMIT),
        name="moe_ffn",
    )(xe, gv, w_gate, w_up, w_down)


WKV_CHUNK = 64
HEAD = RW_HEAD
PAIR = 2 * HEAD


def _bdot(a, b):
    return jnp.dot(a.astype(jnp.bfloat16), b.astype(jnp.bfloat16), preferred_element_type=jnp.float32)


def _bdot_nt(a, b):
    return lax.dot_general(a.astype(jnp.bfloat16), b.astype(jnp.bfloat16), (((1,), (1,)), ((), ())),
                           preferred_element_type=jnp.float32)


def _bdot_tn(a, b):
    return lax.dot_general(a.astype(jnp.bfloat16), b.astype(jnp.bfloat16), (((0,), (0,)), ((), ())),
                           preferred_element_type=jnp.float32)


def _split3(x):
    f32, bf = jnp.float32, jnp.bfloat16
    hi = x.astype(bf)
    r1 = x - hi.astype(f32)
    mid = r1.astype(bf)
    lo = (r1 - mid.astype(f32)).astype(bf)
    return hi, mid, lo


def _wkv_chunk(rs, lws, ks, vs, kks, kkas, ss, rev):
    C = WKV_CHUNK
    f32, bf = jnp.float32, jnp.bfloat16
    n = len(rs)
    ri = lax.broadcasted_iota(jnp.int32, (C, C), 0)
    ci = lax.broadcasted_iota(jnp.int32, (C, C), 1)
    dd = jnp.where(rev == 0, ri - ci, ci - ri)
    tri = (dd >= 0).astype(bf)
    lane = lax.broadcasted_iota(jnp.int32, (C, PAIR), 1)
    first = lane < HEAD
    r2 = lax.broadcasted_iota(jnp.int32, (2 * C, 2 * C), 0)
    c2 = lax.broadcasted_iota(jnp.int32, (2 * C, 2 * C), 1)
    d2 = jnp.where(rev == 0, r2 - c2, c2 - r2)
    strict = d2 > 0
    incl = d2 >= 0
    eye = (r2 == c2).astype(f32)

    def stack(x):
        return jnp.concatenate([jnp.where(first, x, 0.0), jnp.where(first, 0.0, x)], axis=0)

    cums = []
    for lw in lws:
        hi, mid, lo = _split3(lw)
        cums.append(jnp.dot(tri, hi, preferred_element_type=f32) + jnp.dot(tri, mid, preferred_element_type=f32)
                    + jnp.dot(tri, lo, preferred_element_type=f32))
    galls = [jnp.exp(jnp.sum(lw, axis=0, keepdims=True)) for lw in lws]
    lhs, rhs, vS, a2k2 = [], [], [], []
    for i in range(n):
        cum, lw = cums[i], lws[i]
        ginv = jnp.exp(-cum)
        bt = stack(kks[i] * jnp.exp(cum - lw))
        rt = stack(rs[i] * jnp.exp(cum))
        at = stack(-(kkas[i] * ginv))
        kt = stack(ks[i] * ginv)
        lhs.append(jnp.concatenate([bt, rt], axis=0).astype(bf))
        rhs.append(jnp.concatenate([at, kt], axis=0).astype(bf))
        a2k2.append((jnp.concatenate([at, kt], axis=0) * galls[i]).astype(bf))
        vS.append(stack(vs[i]).astype(bf))
    gram = [_bdot_nt(lhs[i], rhs[i]) for i in range(n)]
    C2 = 2 * C
    mab = [jnp.where(strict, g[:C2, :C2], 0.0) for g in gram]
    mbk = [jnp.where(strict, g[:C2, C2:], 0.0).astype(bf) for g in gram]
    mrr = [jnp.concatenate([jnp.where(incl, g[C2:, :C2], 0.0), jnp.where(incl, g[C2:, C2:], 0.0)], axis=1).astype(bf)
           for g in gram]
    ts = [eye + m for m in mab]
    ps = mab
    for _ in range(5):
        ps = [_bdot(p, p) for p in ps]
        ts = [t + _bdot(t, p) for t, p in zip(ts, ps)]
    sB = [s.astype(bf) for s in ss]
    hs = [_bdot_nt(lhs[i], sB[i]) for i in range(n)]
    wS = [hs[i][:C2] + _bdot(mbk[i], vS[i]) for i in range(n)]
    uS = [_bdot(ts[i], wS[i]).astype(bf) for i in range(n)]
    uv = [jnp.concatenate([uS[i], vS[i]], axis=0) for i in range(n)]
    yS = [hs[i][C2:] + _bdot(mrr[i], uv[i]) for i in range(n)]
    ys = [y[:C] + y[C:] for y in yS]
    sn = [ss[i] * galls[i] + _bdot_tn(uv[i], a2k2[i]) for i in range(n)]
    return ys, sn


HALO = 2 * SUBLANES


def _projconv_kernel(x_ref, h_ref, w_ref, ws_ref, o_ref):
    w = w_ref[...]
    p = jnp.dot(x_ref[...], w, preferred_element_type=jnp.float32)
    ph = jnp.dot(h_ref[0], w, preferred_element_type=jnp.float32)
    tm = p.shape[0]
    rid = lax.broadcasted_iota(jnp.int32, (tm, 1), 0)
    prev = jnp.where(rid == 0, ph[SUBLANES - 1:SUBLANES, :], pltpu.roll(p, 1, 0))
    nxt = jnp.where(rid == tm - 1, ph[SUBLANES:SUBLANES + 1, :], pltpu.roll(p, tm - 1, 0))
    o_ref[...] = ws_ref[0:1, :] * prev + ws_ref[1:2, :] * p + ws_ref[2:3, :] * nxt


def _halo_rows(xb, tm):
    B, L, K = xb.shape
    nt = L // tm
    xr = xb.reshape(B, nt, tm, K)
    zero = jnp.zeros((B, 1, SUBLANES, K), xb.dtype)
    above = jnp.concatenate([zero, xr[:, :-1, tm - SUBLANES:]], axis=1)
    below = jnp.concatenate([xr[:, 1:, :SUBLANES], zero], axis=1)
    return jnp.concatenate([above, below], axis=2).reshape(B * nt, HALO, K)


def proj_conv(xb, w, ws, tm=1024, tn=512):
    B, L, K = xb.shape
    N = w.shape[1]
    tm = min(tm, L)
    assert L % tm == 0 and N % tn == 0
    halo = _halo_rows(xb, tm)
    T = B * L
    out = pl.pallas_call(
        _projconv_kernel,
        grid=(T // tm, N // tn),
        in_specs=[pl.BlockSpec((tm, K), lambda i, j: (i, 0)),
                  pl.BlockSpec((1, HALO, K), lambda i, j: (i, 0, 0)),
                  pl.BlockSpec((K, tn), lambda i, j: (0, j)),
                  pl.BlockSpec((3, tn), lambda i, j: (0, j))],
        out_specs=pl.BlockSpec((tm, tn), lambda i, j: (i, j)),
        out_shape=jax.ShapeDtypeStruct((T, N), jnp.float32),
        compiler_params=pltpu.CompilerParams(dimension_semantics=("arbitrary", "arbitrary"), vmem_limit_bytes=VMEM_LIMIT),
        name="proj_conv",
    )(xb.reshape(T, K), halo, w, ws)
    return out.reshape(B, L, N)


def _dft_mats(L):
    N = 2 * L
    f = lax.broadcasted_iota(jnp.int32, (L, L), 0)
    s = lax.broadcasted_iota(jnp.int32, (L, L), 1)
    ang = ((f * s) % N).astype(jnp.float32) * (2.0 * math.pi / N)
    a = jnp.cos(ang)
    sn = jnp.sin(ang)
    alt_s = (1 - 2 * (s % 2)).astype(jnp.float32)
    alt_f = (1 - 2 * (f % 2)).astype(jnp.float32)
    b = jnp.where(f == 0, alt_s, sn)
    b2 = jnp.where(s == 0, alt_f, sn)
    return a.astype(jnp.bfloat16), b.astype(jnp.bfloat16), b2.astype(jnp.bfloat16)


def _hy_fwd_kernel(z_ref, a_ref, b_ref, kp_ref, kq_ref, yp_ref, yq_ref, zb_ref, *, tf):
    j = pl.program_id(2)

    @pl.when(j == 0)
    def _():
        zb_ref[...] = z_ref[0].astype(jnp.bfloat16)

    zb = zb_ref[...]
    p = jnp.dot(a_ref[...], zb, preferred_element_type=jnp.float32)
    q = jnp.dot(b_ref[...], zb, preferred_element_type=jnp.float32)
    kp = kp_ref[...]
    kq = kq_ref[...]
    row = lax.broadcasted_iota(jnp.int32, p.shape, 0) + j * tf
    packed = row == 0
    kqq = kq * q
    yp_ref[0] = (kp * p - jnp.where(packed, 0.0, kqq)).astype(yp_ref.dtype)
    yq_ref[0] = jnp.where(packed, kqq, kp * q + kq * p).astype(yq_ref.dtype)


def _hy_fwd(z, zcol, a, b, kp, kq, tf=512, tc=512):
    B, L, _ = z.shape
    C = kp.shape[1]
    tf = min(tf, L)
    zspec = pl.BlockSpec((1, L, tc), lambda b_, c, j: (b_, 0, zcol + c))
    mspec = pl.BlockSpec((tf, L), lambda b_, c, j: (j, 0))
    kspec = pl.BlockSpec((tf, tc), lambda b_, c, j: (j, c))
    ospec = pl.BlockSpec((1, tf, tc), lambda b_, c, j: (b_, j, c))
    return pl.pallas_call(
        functools.partial(_hy_fwd_kernel, tf=tf),
        grid=(B, C // tc, L // tf),
        in_specs=[zspec, mspec, mspec, kspec, kspec],
        out_specs=[ospec, ospec],
        out_shape=[jax.ShapeDtypeStruct((B, L, C), jnp.bfloat16)] * 2,
        scratch_shapes=[pltpu.VMEM((L, tc), jnp.bfloat16)],
        compiler_params=pltpu.CompilerParams(dimension_semantics=("arbitrary",) * 3, vmem_limit_bytes=VMEM_LIMIT),
        name="hy_fwd",
    )(z, a, b, kp, kq)


def _hy_inv_kernel(yp_ref, yq_ref, a_ref, b2_ref, z_ref, x_ref, db_ref, o_ref):
    y = jnp.dot(a_ref[...], yp_ref[0], preferred_element_type=jnp.float32)
    y += jnp.dot(b2_ref[...], yq_ref[0], preferred_element_type=jnp.float32)
    o_ref[0] = (x_ref[0] * (y + z_ref[0] * db_ref[...])).astype(o_ref.dtype)


def _hy_inv(yp, yq, a, b2, zarr, zcol, xarr, xcol, db, out_dtype, tt=512, tc=512):
    B, L, C = yp.shape
    tt = min(tt, L)
    yspec = pl.BlockSpec((1, L, tc), lambda b_, c, i: (b_, 0, c))
    mspec = pl.BlockSpec((tt, L), lambda b_, c, i: (i, 0))
    return pl.pallas_call(
        _hy_inv_kernel,
        grid=(B, C // tc, L // tt),
        in_specs=[yspec, yspec, mspec, mspec,
                  pl.BlockSpec((1, tt, tc), lambda b_, c, i: (b_, i, zcol + c)),
                  pl.BlockSpec((1, tt, tc), lambda b_, c, i: (b_, i, xcol + c)),
                  pl.BlockSpec((1, tc), lambda b_, c, i: (0, c))],
        out_specs=pl.BlockSpec((1, tt, tc), lambda b_, c, i: (b_, i, c)),
        out_shape=jax.ShapeDtypeStruct((B, L, C), out_dtype),
        compiler_params=pltpu.CompilerParams(dimension_semantics=("arbitrary",) * 3, vmem_limit_bytes=VMEM_LIMIT),
        name="hy_inv",
    )(yp, yq, a, b2, zarr, xarr, db)


def _hy_spectra(filt, a, b, mm):
    L = filt.shape[0]
    N = 2 * L
    row0 = (jnp.arange(L) == 0)[:, None]
    hs = []
    for o in range(filt.shape[1]):
        hs += [filt[:, o, 0], jnp.where(row0, 0.0, filt[:, o, 1])]
    hcat = jnp.concatenate(hs, axis=1).astype(jnp.bfloat16)
    cp = mm(a, hcat)
    sp = mm(b, hcat)
    C = filt.shape[3]
    wgt = jnp.where(row0, 1.0 / N, 2.0 / N)
    out = []
    for o in range(filt.shape[1]):
        cf, cb = cp[:, (2 * o) * C:(2 * o + 1) * C], cp[:, (2 * o + 1) * C:(2 * o + 2) * C]
        sf, sb = sp[:, (2 * o) * C:(2 * o + 1) * C], sp[:, (2 * o + 1) * C:(2 * o + 2) * C]
        out.append(((cf + cb) * wgt, jnp.where(row0, sf + sb, sf - sb) * wgt))
    return out


def hyena(sc, filt, dbias, mm, tc=512):
    L = sc.shape[1]
    C = filt.shape[3]
    a, b, b2 = _dft_mats(L)
    (kp1, kq1), (kp2, kq2) = _hy_spectra(filt, a, b, mm)
    db = dbias.astype(jnp.float32)
    nb = C // tc
    yp, yq = _hy_fwd(sc, 0, a, b, kp1, kq1, tc=tc)
    z2 = _hy_inv(yp, yq, a, b2, sc, 0, sc, nb, db[0:1], jnp.float32, tc=tc)
    yp, yq = _hy_fwd(z2, 0, a, b, kp2, kq2, tc=tc)
    return _hy_inv(yp, yq, a, b2, z2, 0, sc, 2 * nb, db[1:2], jnp.bfloat16, tc=tc)


def _split_dot(x, ones_bf16):
    hi = x.astype(jnp.bfloat16)
    lo = (x - hi.astype(jnp.float32)).astype(jnp.bfloat16)
    return (jnp.dot(hi, ones_bf16, preferred_element_type=jnp.float32)
            + jnp.dot(lo, ones_bf16, preferred_element_type=jnp.float32))


def _head_ones():
    r = lax.broadcasted_iota(jnp.int32, (PAIR, PAIR), 0) // HEAD
    c = lax.broadcasted_iota(jnp.int32, (PAIR, PAIR), 1) // HEAD
    return (r == c).astype(jnp.bfloat16)


def _group_sum(x, ones):
    return jnp.concatenate([_split_dot(x[:, p * PAIR:(p + 1) * PAIR], ones) for p in range(x.shape[1] // PAIR)], axis=1)


def _softplus(x):
    return jnp.maximum(x, 0.0) + jnp.log1p(jnp.exp(-jnp.abs(x)))


def _wkv_fused_kernel(r_ref, k_ref, v_ref, lo_ref, w0_ref, a0_ref, w2_ref, a2_ref, kk_ref, ka_ref, o_ref, h_ref):
    rev = pl.program_id(0)

    @pl.when(pl.program_id(2) == 0)
    def _():
        h_ref[...] = jnp.zeros_like(h_ref)

    f32 = jnp.float32
    r = r_ref[0]
    k = k_ref[0]
    v = v_ref[0]
    xw = lo_ref[0, :, 0:PAIR]
    xa = lo_ref[0, :, PAIR:2 * PAIR]
    dw = jnp.dot(jnp.tanh(xw).astype(jnp.bfloat16), w2_ref[0], preferred_element_type=f32)
    lw = -jnp.exp(-_softplus(-(w0_ref[0] + dw)) - 0.5)
    a = jax.nn.sigmoid(a0_ref[0] + jnp.dot(xa.astype(jnp.bfloat16), a2_ref[0], preferred_element_type=f32))
    kq = k * kk_ref[...]
    ones = _head_ones()
    sq = (kq * kq).astype(jnp.bfloat16)
    ss = jnp.concatenate([jnp.dot(sq[:, p * PAIR:(p + 1) * PAIR], ones, preferred_element_type=f32)
                          for p in range(D_RW // PAIR)], axis=1)
    kk = kq / jnp.maximum(jnp.sqrt(ss), 1e-12)
    kdir = k * (1.0 + (a - 1.0) * ka_ref[...])
    kka = kk * a
    n = D_RW // PAIR
    sls = [slice(p * PAIR, (p + 1) * PAIR) for p in range(n)]
    ys, hn = _wkv_chunk([r[:, s] for s in sls], [lw[:, s] for s in sls], [kdir[:, s] for s in sls],
                        [v[:, s] for s in sls], [kk[:, s] for s in sls], [kka[:, s] for s in sls],
                        [h_ref[p] for p in range(n)], rev)
    for p in range(n):
        o_ref[0, 0, :, sls[p]] = ys[p]
        h_ref[p] = hn[p]


def wkv7_fused(sc, rcol, lcol, w0, a0, w2p, a2p, k_k, k_a):
    B, L, _ = sc.shape
    C = WKV_CHUNK
    nc = L // C
    D = D_RW
    assert L % C == 0
    cm = lambda d, c: c + d * (nc - 1 - 2 * c)
    col = lambda j: pl.BlockSpec((1, C, D), lambda d, b, c: (b, cm(d, c), j))
    vec2 = pl.BlockSpec((1, 1, D), lambda d, b, c: (d, 0, 0))
    mat2 = pl.BlockSpec((1, PAIR, D), lambda d, b, c: (d, 0, 0))
    vec = pl.BlockSpec((1, D), lambda d, b, c: (0, 0))
    return pl.pallas_call(
        _wkv_fused_kernel,
        grid=(2, B, nc),
        in_specs=[col(rcol), col(rcol + 1), col(rcol + 2),
                  pl.BlockSpec((1, C, LORA_W), lambda d, b, c: (b, cm(d, c), lcol)),
                  vec2, vec2, mat2, mat2, vec, vec],
        out_specs=pl.BlockSpec((1, 1, C, D), lambda d, b, c: (d, b, cm(d, c), 0)),
        out_shape=jax.ShapeDtypeStruct((2, B, L, D), jnp.float32),
        scratch_shapes=[pltpu.VMEM((D // PAIR, PAIR, PAIR), jnp.float32)],
        compiler_params=pltpu.CompilerParams(dimension_semantics=("arbitrary",) * 3, vmem_limit_bytes=VMEM_LIMIT),
        name="wkv7",
    )(sc, sc, sc, sc, w0, a0, w2p, a2p, k_k, k_a)


def _rw_post_kernel(y_ref, r_ref, k_ref, v_ref, lo_ref, a0_ref, a2_ref, ka_ref, rk_ref, lnw_ref, lnb_ref, g2_ref, wo_ref, o_ref):
    f32 = jnp.float32
    ones = _head_ones()
    y = y_ref[0, 0] + y_ref[1, 0]
    mu = _group_sum(y, ones) * (1.0 / HEAD)
    yc = y - mu
    var = _group_sum(yc * yc, ones) * (1.0 / HEAD)
    out = yc * lax.rsqrt(var + RW_GN_EPS) * lnw_ref[...] + lnb_ref[...]
    r = r_ref[0]
    k = k_ref[0]
    v = v_ref[0]
    xa = lo_ref[0, :, PAIR:2 * PAIR].astype(jnp.bfloat16)
    for d in range(2):
        a = jax.nn.sigmoid(a0_ref[d] + jnp.dot(xa, a2_ref[d], preferred_element_type=f32))
        kdir = k * (1.0 + (a - 1.0) * ka_ref[...])
        out = out + _group_sum(r * kdir * rk_ref[d], ones) * v
    g = jnp.dot(jax.nn.sigmoid(lo_ref[0]).astype(jnp.bfloat16), g2_ref[...], preferred_element_type=f32)
    o_ref[0] = jnp.dot((out * g).astype(jnp.bfloat16), wo_ref[...], preferred_element_type=f32).astype(o_ref.dtype)


def rw_post(y2, sc, rcol, lcol, a0, a2p, k_a, r_k, ln_w, ln_b, g2p, w_o, tm=256):
    _, B, L, D = y2.shape
    DM = w_o.shape[1]
    tm = min(tm, L)
    col = lambda j: pl.BlockSpec((1, tm, D), lambda b, i: (b, i, j))
    full = lambda shape: pl.BlockSpec(shape, lambda b, i: (0,) * len(shape))
    return pl.pallas_call(
        _rw_post_kernel,
        grid=(B, L // tm),
        in_specs=[pl.BlockSpec((2, 1, tm, D), lambda b, i: (0, b, i, 0)),
                  col(rcol), col(rcol + 1), col(rcol + 2),
                  pl.BlockSpec((1, tm, LORA_W), lambda b, i: (b, i, lcol)),
                  full((2, 1, D)), full((2, PAIR, D)), full((1, D)), full((2, 1, D)), full((1, D)), full((1, D)),
                  full((LORA_W, D)), full((D, DM))],
        out_specs=pl.BlockSpec((1, tm, DM), lambda b, i: (b, i, 0)),
        out_shape=jax.ShapeDtypeStruct((B, L, DM), jnp.bfloat16),
        compiler_params=pltpu.CompilerParams(dimension_semantics=("arbitrary",) * 2, vmem_limit_bytes=VMEM_LIMIT),
        name="rw_post",
    )(y2, sc, sc, sc, sc, a0, a2p, k_a, r_k, ln_w, ln_b, g2p, w_o)


def rwkv_params(p):
    f32, bf = jnp.float32, jnp.bfloat16
    z64 = jnp.zeros((64, D_RW), f32)
    w2p = jnp.stack([jnp.concatenate([p['rw_w2'][0], z64]), jnp.concatenate([z64, p['rw_w2'][1]])]).astype(bf)
    a2p = jnp.stack([jnp.concatenate([p['rw_a2'][0], z64]), jnp.concatenate([z64, p['rw_a2'][1]])]).astype(bf)
    g2p = jnp.zeros((LORA_W, D_RW), f32).at[256:256 + 160].set(p['rw_g2']).astype(bf)
    return dict(w0=p['rw_w0'].reshape(2, 1, D_RW), a0=p['rw_a0'].reshape(2, 1, D_RW), w2p=w2p, a2p=a2p, g2p=g2p,
                k_k=p['rw_k_k'].reshape(1, D_RW), k_a=p['rw_k_a'].reshape(1, D_RW), r_k=p['rw_r_k'].reshape(2, 1, D_RW),
                ln_w=p['rw_ln_w'].reshape(1, D_RW), ln_b=p['rw_ln_b'].reshape(1, D_RW), w_o=p['rw_w_o'].astype(bf))


def rwkv(sc, rcol, lcol, q):
    y2 = wkv7_fused(sc, rcol, lcol, q['w0'], q['a0'], q['w2p'], q['a2p'], q['k_k'], q['k_a'])
    return rw_post(y2, sc, rcol, lcol, q['a0'], q['a2p'], q['k_a'], q['r_k'], q['ln_w'], q['ln_b'], q['g2p'], q['w_o'])


def _attn_kernel(q_ref, k_ref, v_ref, wo_ref, o_ref):
    hp = pl.program_id(2)

    @pl.when(hp == 0)
    def _():
        o_ref[...] = jnp.zeros_like(o_ref)

    outs = []
    for h in range(CB // CA_HEAD):
        sl = slice(h * CA_HEAD, (h + 1) * CA_HEAD)
        q = q_ref[0, :, sl].astype(jnp.bfloat16)
        s = lax.dot_general(q, k_ref[0, :, sl].astype(jnp.bfloat16), (((1,), (1,)), ((), ())),
                            preferred_element_type=jnp.float32) * (CA_HEAD ** -0.5)
        s = s - jnp.max(s, axis=-1, keepdims=True)
        e = jnp.exp(s)
        p = e / jnp.sum(e, axis=-1, keepdims=True)
        outs.append(jnp.dot(p.astype(jnp.bfloat16), v_ref[0, :, sl].astype(jnp.bfloat16), preferred_element_type=jnp.float32))
    att = jnp.concatenate(outs, axis=1).astype(jnp.bfloat16)
    o_ref[0] += jnp.dot(att, wo_ref[...], preferred_element_type=jnp.float32)


def mem_attention(proj, qcol, kv, w_o, tm=512):
    B, L, _ = proj.shape
    M = kv.shape[1]
    DM = w_o.shape[1]
    nhp = w_o.shape[0] // CB
    tm = min(tm, L)
    return pl.pallas_call(
        _attn_kernel,
        grid=(B, L // tm, nhp),
        in_specs=[pl.BlockSpec((1, tm, CB), lambda b, i, h: (b, i, qcol + h)),
                  pl.BlockSpec((1, M, CB), lambda b, i, h: (b, 0, h)),
                  pl.BlockSpec((1, M, CB), lambda b, i, h: (b, 0, nhp + h)),
                  pl.BlockSpec((CB, DM), lambda b, i, h: (h, 0))],
        out_specs=pl.BlockSpec((1, tm, DM), lambda b, i, h: (b, i, 0)),
        out_shape=jax.ShapeDtypeStruct((B, L, DM), jnp.float32),
        compiler_params=pltpu.CompilerParams(dimension_semantics=("arbitrary",) * 3, vmem_limit_bytes=VMEM_LIMIT),
        name="mem_attn",
    )(proj, kv, kv, w_o)


def _merge_kernel(hy_ref, rw_ref, ca_ref, g0_ref, g1_ref, g2_ref, wo_ref, x_ref, lg_ref, lb_ref, wr_ref,
                  o_ref, ob_ref, lo_ref, acc_ref):
    kc = pl.program_id(2)

    @pl.when(kc == 0)
    def _():
        acc_ref[...] = jnp.zeros_like(acc_ref)

    f32 = jnp.float32
    m = (jax.nn.sigmoid(g0_ref[0].astype(f32)) * hy_ref[0] + jax.nn.sigmoid(g1_ref[0].astype(f32)) * rw_ref[0]
         + jax.nn.sigmoid(g2_ref[0].astype(f32)) * ca_ref[0])
    acc_ref[...] += jnp.dot(m.astype(jnp.bfloat16), wo_ref[...], preferred_element_type=jnp.float32)

    @pl.when(kc == pl.num_programs(2) - 1)
    def _():
        h = DN_ALPHA * x_ref[0] + acc_ref[...]
        mu = jnp.mean(h, axis=-1, keepdims=True)
        hc = h - mu
        var = jnp.mean(hc * hc, axis=-1, keepdims=True)
        y = hc * lax.rsqrt(var + LN_EPS) * lg_ref[...] + lb_ref[...]
        o_ref[0] = y
        yb = y.astype(jnp.bfloat16)
        ob_ref[0] = yb
        lo_ref[0] = jnp.dot(yb, wr_ref[...], preferred_element_type=jnp.float32)


def merge_ln(hy, rw, ca, proj, gcol, w_out, x, ln_g, ln_b, w_router_pad, tm=512):
    B, L, DM = x.shape
    nk = DM // CB
    tm = min(tm, L)
    br = pl.BlockSpec((1, tm, CB), lambda b, i, k: (b, i, k))
    gate = lambda g: pl.BlockSpec((1, tm, CB), lambda b, i, k: (b, i, gcol + g * nk + k))
    row = pl.BlockSpec((1, tm, DM), lambda b, i, k: (b, i, 0))
    vec = pl.BlockSpec((1, DM), lambda b, i, k: (0, 0))
    return pl.pallas_call(
        _merge_kernel,
        grid=(B, L // tm, nk),
        in_specs=[br, br, br, gate(0), gate(1), gate(2),
                  pl.BlockSpec((CB, DM), lambda b, i, k: (k, 0)), row, vec, vec,
                  pl.BlockSpec((DM, LANES), lambda b, i, k: (0, 0))],
        out_specs=[row, row, pl.BlockSpec((1, tm, LANES), lambda b, i, k: (b, i, 0))],
        out_shape=[jax.ShapeDtypeStruct((B, L, DM), jnp.float32), jax.ShapeDtypeStruct((B, L, DM), jnp.bfloat16),
                   jax.ShapeDtypeStruct((B, L, LANES), jnp.float32)],
        scratch_shapes=[pltpu.VMEM((tm, DM), jnp.float32)],
        compiler_params=pltpu.CompilerParams(dimension_semantics=("arbitrary",) * 3, vmem_limit_bytes=VMEM_LIMIT),
        name="merge_ln",
    )(hy, rw, ca, proj, proj, proj, w_out, x, ln_g, ln_b, w_router_pad)


def _ln_kernel(x_ref, m_ref, g_ref, b_ref, o_ref):
    h = DN_ALPHA * x_ref[...] + m_ref[...]
    mu = jnp.mean(h, axis=-1, keepdims=True)
    hc = h - mu
    var = jnp.mean(hc * hc, axis=-1, keepdims=True)
    o_ref[...] = hc * lax.rsqrt(var + LN_EPS) * g_ref[...] + b_ref[...]


def resid_ln(x, m, g, b, tm=512):
    T, DM = x.shape
    row = pl.BlockSpec((tm, DM), lambda i: (i, 0))
    vec = pl.BlockSpec((1, DM), lambda i: (0, 0))
    return pl.pallas_call(
        _ln_kernel, grid=(T // tm,), in_specs=[row, row, vec, vec], out_specs=row,
        out_shape=jax.ShapeDtypeStruct((T, DM), jnp.float32),
        compiler_params=pltpu.CompilerParams(dimension_semantics=("arbitrary",), vmem_limit_bytes=VMEM_LIMIT),
        name="resid_ln",
    )(x, m, g, b)


def _hyena_filters(L, w1, b1, w2, b2, w3, b3, w4, sin_freq):
    f32 = jnp.float32
    t = jnp.linspace(0.0, 1.0, L, dtype=f32)[:, None]
    ang = 2.0 * math.pi * jnp.arange(L, dtype=f32)[:, None] / L
    bands = jnp.linspace(1e-4, HY_BANDS - 1, HY_BANDS, dtype=f32)[None]
    z = jnp.concatenate([t, jnp.cos(bands * ang), -jnp.sin(bands * ang)], axis=-1)
    sf = sin_freq.astype(f32)
    h = jnp.sin(sf[0] * (z @ w1.astype(f32) + b1.astype(f32)))
    h = jnp.sin(sf[1] * (h @ w2.astype(f32) + b2.astype(f32)))
    h = jnp.sin(sf[2] * (h @ w3.astype(f32) + b3.astype(f32)))
    h = (h @ w4.astype(f32)).reshape(L, HY_ORDER, 2, D_HY)
    deltas = jnp.abs(jnp.linspace(math.log(HY_DECAY_TARGET) / HY_SLOW_DECAY_PCT,
                                  math.log(HY_DECAY_TARGET) / HY_FAST_DECAY_PCT, D_HY, dtype=f32))
    window = jnp.exp(-t * deltas[None])
    return h * window[:, None, None, :]


def _expert_choice_moe(xt, xb, logits, w_gate, w_up, w_down):
    T, D = xt.shape
    cap = (EC_CAPACITY * T) // N_EXPERTS
    aff = jax.nn.softmax(logits.astype(jnp.float32), axis=-1)
    gval, idx = lax.top_k(aff.T, cap)
    out = _moe_ffn(xb[idx], jnp.broadcast_to(gval[..., None], gval.shape + (LANES,)), w_gate, w_up, w_down)
    return jnp.zeros_like(xt).at[idx.reshape(-1)].add(out.reshape(-1, D))


def _layer(x, mem, p):
    B, L, D = x.shape
    xb = x.astype(jnp.bfloat16)
    sc = proj_conv(xb, p['w_conv'], p['w_short_all'])
    pq = _mm3(xb, p['w_rest'], out_dtype=jnp.bfloat16)
    filt = _hyena_filters(L, p['hy_w1'], p['hy_b1'], p['hy_w2'], p['hy_b2'], p['hy_w3'], p['hy_b3'],
                          p['hy_w4'], p['hy_sin_freq'])
    hy = _mm3(hyena(sc, filt, p['hy_dbias'], _matmul), p['hy_w_o'], out_dtype=jnp.bfloat16)
    rw = rwkv(sc, P_RW // D_RW, P_LORA // LORA_W, p['rw'])
    ca = mem_attention(pq, 0, _mm3(mem, p['ca_w_kv']), p['ca_w_o'])
    x1, x1b, logits = merge_ln(hy, rw, ca, pq, (P_GATE - N_CONV) // CB, p['w_out'], x, p['ln1_g'], p['ln1_b'], p['w_router_pad'])
    moe = _expert_choice_moe(x1.reshape(B * L, D), x1b.reshape(B * L, D), logits.reshape(B * L, LANES)[:, :N_EXPERTS],
                             p['moe_w_gate'], p['moe_w_up'], p['moe_w_down'])
    return resid_ln(x1.reshape(B * L, D), moe, p['ln2_g'], p['ln2_b']).reshape(B, L, D)


def kernel(x_prompt, x_sample, mem_prompt, mem_sample, w_in, w_short, hy_w1, hy_b1, hy_w2, hy_b2, hy_w3, hy_b3, hy_w4, hy_sin_freq, hy_dbias, hy_w_o, rw_w0, rw_w2, rw_a0, rw_a2, rw_g2, rw_k_k, rw_k_a, rw_r_k, rw_ln_w, rw_ln_b, rw_w_o, ca_w_kv, ca_w_o, w_out, ln1_g, ln1_b, moe_w_router, moe_w_gate, moe_w_up, moe_w_down, ln2_g, ln2_b):
    params = dict(w_in=w_in, w_short=w_short, hy_w1=hy_w1, hy_b1=hy_b1, hy_w2=hy_w2, hy_b2=hy_b2,
                  hy_w3=hy_w3, hy_b3=hy_b3, hy_w4=hy_w4, hy_sin_freq=hy_sin_freq, hy_dbias=hy_dbias,
                  hy_w_o=hy_w_o, rw_w0=rw_w0, rw_w2=rw_w2, rw_a0=rw_a0, rw_a2=rw_a2, rw_g2=rw_g2,
                  rw_k_k=rw_k_k, rw_k_a=rw_k_a, rw_r_k=rw_r_k, rw_ln_w=rw_ln_w, rw_ln_b=rw_ln_b,
                  rw_w_o=rw_w_o, ca_w_kv=ca_w_kv, ca_w_o=ca_w_o, w_out=w_out, ln1_g=ln1_g, ln1_b=ln1_b,
                  moe_w_router=moe_w_router, moe_w_gate=moe_w_gate, moe_w_up=moe_w_up,
                  moe_w_down=moe_w_down, ln2_g=ln2_g, ln2_b=ln2_b)
    bf = jnp.bfloat16
    y_prompt, y_sample = x_prompt, x_sample
    for i in range(DEPTH):
        lp = {name: arr[i] for name, arr in params.items()}
        wi = lp['w_in']
        lp['w_conv'] = jnp.pad(wi[:, :OFF_Q], ((0, 0), (0, N_LORA_PAD - N_LORA))).astype(bf)
        lp['w_rest'] = wi[:, OFF_Q:].astype(bf)
        lp['w_short_all'] = jnp.pad(lp['w_short'], ((0, 0), (0, N_LORA_PAD - N_LORA)))
        for name in ('hy_w_o', 'ca_w_o', 'ca_w_kv', 'w_out'):
            lp[name] = lp[name].astype(bf)
        lp['rw'] = rwkv_params(lp)
        lp['w_router_pad'] = jnp.pad(lp['moe_w_router'], ((0, 0), (0, LANES - N_EXPERTS))).astype(bf)
        for name in ('ln1_g', 'ln1_b', 'ln2_g', 'ln2_b'):
            lp[name] = lp[name].reshape(1, D_MODEL)
        y_prompt = _layer(y_prompt, mem_prompt, lp)
        y_sample = _layer(y_sample, mem_sample, lp)
    return (y_prompt, y_sample)
```

```python
import functools
import math

import jax
import jax.numpy as jnp
from jax import lax
from jax.experimental import pallas as pl
from jax.experimental.pallas import tpu as pltpu

D_MODEL = 2048
DEPTH = 1
D_HY = D_MODEL // 2
HY_ORDER = 2
HY_EMB = 33
HY_BANDS = (HY_EMB - 1) // 2
HY_FFN = 64
HY_FAST_DECAY_PCT = 0.3
HY_SLOW_DECAY_PCT = 1.5
HY_DECAY_TARGET = 1e-2
D_RW = D_MODEL // 2
RW_HEAD = 64
RW_HEADS = D_RW // RW_HEAD
RW_DECAY_LORA = 64
RW_AAA_LORA = 64
RW_GATE_LORA = 160
RW_GN_EPS = 64e-5
D_CA = D_MODEL // 2
CA_HEADS = 4
CA_HEAD = D_CA // CA_HEADS
N_BRANCH = 3
N_EXPERTS = 16
EC_CAPACITY = 2
D_EXPERT = 2 * D_MODEL
LN_EPS = 1e-5
DN_ALPHA = (2 * DEPTH) ** 0.25

OFF_HY = 0
OFF_RW = OFF_HY + 3 * D_HY
OFF_LORA = OFF_RW + 3 * D_RW
N_LORA = 2 * RW_DECAY_LORA + 2 * RW_AAA_LORA + RW_GATE_LORA
OFF_Q = OFF_LORA + N_LORA
OFF_GATE = OFF_Q + D_CA
N_IN = OFF_GATE + N_BRANCH * D_MODEL

LANES = 128
N_LORA_PAD = -(-N_LORA // LANES) * LANES
P_HY = 0
P_RW = P_HY + 3 * D_HY
P_LORA = P_RW + 3 * D_RW
P_Q = P_LORA + N_LORA_PAD
P_GATE = P_Q + D_CA
P_ALL = P_GATE + N_BRANCH * D_MODEL

VMEM_LIMIT = 56 * 1024 * 1024
SUBLANES = 8
CB = 512
N_CONV = P_Q
LORA_W = N_LORA_PAD


def _mm_kernel(a_ref, b_ref, o_ref):
    o_ref[...] = jnp.dot(a_ref[...].astype(jnp.bfloat16), b_ref[...].astype(jnp.bfloat16),
                         preferred_element_type=jnp.float32).astype(o_ref.dtype)


def _matmul(a, b, tm=1024, tn=512, out_dtype=jnp.float32):
    M, K = a.shape
    N = b.shape[1]
    tm = min(tm, M)
    tn = min(tn, N)
    assert M % tm == 0 and N % tn == 0, (M, N, tm, tn)
    return pl.pallas_call(
        _mm_kernel,
        grid=(M // tm, N // tn),
        in_specs=[pl.BlockSpec((tm, K), lambda i, j: (i, 0)),
                  pl.BlockSpec((K, tn), lambda i, j: (0, j))],
        out_specs=pl.BlockSpec((tm, tn), lambda i, j: (i, j)),
        out_shape=jax.ShapeDtypeStruct((M, N), out_dtype),
        compiler_params=pltpu.CompilerParams(dimension_semantics=("arbitrary", "arbitrary"),
                                             vmem_limit_bytes=VMEM_LIMIT),
        name="mm",
    )(a, b)


def _mm3(x, w, **kw):
    B, L, K = x.shape
    return _matmul(x.reshape(B * L, K), w, **kw).reshape(B, L, w.shape[1])


def _moe_kernel(x_ref, gv_ref, wg_ref, wu_ref, wd_ref, o_ref):
    @pl.when(pl.program_id(2) == 0)
    def _():
        o_ref[...] = jnp.zeros_like(o_ref)

    x = x_ref[0]
    hg = jnp.dot(x, wg_ref[0].astype(jnp.bfloat16), preferred_element_type=jnp.float32)
    hu = jnp.dot(x, wu_ref[0].astype(jnp.bfloat16), preferred_element_type=jnp.float32)
    h = (hg * jax.nn.sigmoid(hg) * hu).astype(jnp.bfloat16)
    o_ref[0] += jnp.dot(h, wd_ref[0].astype(jnp.bfloat16), preferred_element_type=jnp.float32)

    @pl.when(pl.program_id(2) == pl.num_programs(2) - 1)
    def _():
        o_ref[0] = o_ref[0] * jnp.tile(gv_ref[0], (1, o_ref.shape[2] // LANES))


def _moe_ffn(xe, gv, w_gate, w_up, w_down, tm=1024, tf=256):
    E, C, D = xe.shape
    F = w_gate.shape[2]
    tm = min(tm, C)
    return pl.pallas_call(
        _moe_kernel,
        grid=(E, C // tm, F // tf),
        in_specs=[pl.BlockSpec((1, tm, D), lambda e, i, f: (e, i, 0)),
                  pl.BlockSpec((1, tm, LANES), lambda e, i, f: (e, i, 0)),
                  pl.BlockSpec((1, D, tf), lambda e, i, f: (e, 0, f)),
                  pl.BlockSpec((1, D, tf), lambda e, i, f: (e, 0, f)),
                  pl.BlockSpec((1, tf, D), lambda e, i, f: (e, f, 0))],
        out_specs=pl.BlockSpec((1, tm, D), lambda e, i, f: (e, i, 0)),
        out_shape=jax.ShapeDtypeStruct((E, C, D), jnp.float32),
        compiler_params=pltpu.CompilerParams(dimension_semantics=("arbitrary",) * 3, vmem_limit_bytes=VMEM_LIMIT),
        name="moe_ffn",
    )(xe, gv, w_gate, w_up, w_down)


WKV_CHUNK = 64
HEAD = RW_HEAD
PAIR = 2 * HEAD


def _bdot(a, b):
    return jnp.dot(a.astype(jnp.bfloat16), b.astype(jnp.bfloat16), preferred_element_type=jnp.float32)


def _bdot_nt(a, b):
    return lax.dot_general(a.astype(jnp.bfloat16), b.astype(jnp.bfloat16), (((1,), (1,)), ((), ())),
                           preferred_element_type=jnp.float32)


def _bdot_tn(a, b):
    return lax.dot_general(a.astype(jnp.bfloat16), b.astype(jnp.bfloat16), (((0,), (0,)), ((), ())),
                           preferred_element_type=jnp.float32)


def _split3(x):
    f32, bf = jnp.float32, jnp.bfloat16
    hi = x.astype(bf)
    r1 = x - hi.astype(f32)
    mid = r1.astype(bf)
    lo = (r1 - mid.astype(f32)).astype(bf)
    return hi, mid, lo


def _wkv_chunk(rs, lws, ks, vs, kks, kkas, ss, rev):
    C = WKV_CHUNK
    f32, bf = jnp.float32, jnp.bfloat16
    n = len(rs)
    ri = lax.broadcasted_iota(jnp.int32, (C, C), 0)
    ci = lax.broadcasted_iota(jnp.int32, (C, C), 1)
    dd = jnp.where(rev == 0, ri - ci, ci - ri)
    tri = (dd >= 0).astype(bf)
    lane = lax.broadcasted_iota(jnp.int32, (C, PAIR), 1)
    first = lane < HEAD
    r2 = lax.broadcasted_iota(jnp.int32, (2 * C, 2 * C), 0)
    c2 = lax.broadcasted_iota(jnp.int32, (2 * C, 2 * C), 1)
    d2 = jnp.where(rev == 0, r2 - c2, c2 - r2)
    strict = d2 > 0
    incl = d2 >= 0
    eye = (r2 == c2).astype(f32)

    def stack(x):
        return jnp.concatenate([jnp.where(first, x, 0.0), jnp.where(first, 0.0, x)], axis=0)

    cums = []
    for lw in lws:
        hi, mid, lo = _split3(lw)
        cums.append(jnp.dot(tri, hi, preferred_element_type=f32) + jnp.dot(tri, mid, preferred_element_type=f32)
                    + jnp.dot(tri, lo, preferred_element_type=f32))
    galls = [jnp.exp(jnp.sum(lw, axis=0, keepdims=True)) for lw in lws]
    lhs, rhs, vS, a2k2 = [], [], [], []
    for i in range(n):
        cum, lw = cums[i], lws[i]
        ginv = jnp.exp(-cum)
        bt = stack(kks[i] * jnp.exp(cum - lw))
        rt = stack(rs[i] * jnp.exp(cum))
        at = stack(-(kkas[i] * ginv))
        kt = stack(ks[i] * ginv)
        lhs.append(jnp.concatenate([bt, rt], axis=0).astype(bf))
        rhs.append(jnp.concatenate([at, kt], axis=0).astype(bf))
        a2k2.append((jnp.concatenate([at, kt], axis=0) * galls[i]).astype(bf))
        vS.append(stack(vs[i]).astype(bf))
    gram = [_bdot_nt(lhs[i], rhs[i]) for i in range(n)]
    C2 = 2 * C
    mab = [jnp.where(strict, g[:C2, :C2], 0.0) for g in gram]
    mbk = [jnp.where(strict, g[:C2, C2:], 0.0).astype(bf) for g in gram]
    mrr = [jnp.concatenate([jnp.where(incl, g[C2:, :C2], 0.0), jnp.where(incl, g[C2:, C2:], 0.0)], axis=1).astype(bf)
           for g in gram]
    ts = [eye + m for m in mab]
    ps = [_bdot(m, m) for m in mab]
    for _ in range(4):
        xs = [_bdot(jnp.concatenate([t, p], axis=0), p) for t, p in zip(ts, ps)]
        ts = [t + x[:C2] for t, x in zip(ts, xs)]
        ps = [x[C2:] for x in xs]
    ts = [t + _bdot(t, p) for t, p in zip(ts, ps)]
    sB = [s.astype(bf) for s in ss]
    hs = [_bdot_nt(lhs[i], sB[i]) for i in range(n)]
    wS = [hs[i][:C2] + _bdot(mbk[i], vS[i]) for i in range(n)]
    uS = [_bdot(ts[i], wS[i]).astype(bf) for i in range(n)]
    uv = [jnp.concatenate([uS[i], vS[i]], axis=0) for i in range(n)]
    yS = [hs[i][C2:] + _bdot(mrr[i], uv[i]) for i in range(n)]
    ys = [y[:C] + y[C:] for y in yS]
    sn = [ss[i] * galls[i] + _bdot_tn(uv[i], a2k2[i]) for i in range(n)]
    return ys, sn


HALO = 2 * SUBLANES


def _projconv_kernel(x_ref, h_ref, w_ref, ws_ref, o_ref):
    w = w_ref[...]
    p = jnp.dot(x_ref[...], w, preferred_element_type=jnp.float32)
    ph = jnp.dot(h_ref[0], w, preferred_element_type=jnp.float32)
    tm = p.shape[0]
    rid = lax.broadcasted_iota(jnp.int32, (tm, 1), 0)
    prev = jnp.where(rid == 0, ph[SUBLANES - 1:SUBLANES, :], pltpu.roll(p, 1, 0))
    nxt = jnp.where(rid == tm - 1, ph[SUBLANES:SUBLANES + 1, :], pltpu.roll(p, tm - 1, 0))
    o_ref[...] = ws_ref[0:1, :] * prev + ws_ref[1:2, :] * p + ws_ref[2:3, :] * nxt


def _halo_rows(xb, tm):
    B, L, K = xb.shape
    nt = L // tm
    xr = xb.reshape(B, nt, tm, K)
    zero = jnp.zeros((B, 1, SUBLANES, K), xb.dtype)
    above = jnp.concatenate([zero, xr[:, :-1, tm - SUBLANES:]], axis=1)
    below = jnp.concatenate([xr[:, 1:, :SUBLANES], zero], axis=1)
    return jnp.concatenate([above, below], axis=2).reshape(B * nt, HALO, K)


def proj_conv(xb, w, ws, tm=1024, tn=512):
    B, L, K = xb.shape
    N = w.shape[1]
    tm = min(tm, L)
    assert L % tm == 0 and N % tn == 0
    halo = _halo_rows(xb, tm)
    T = B * L
    out = pl.pallas_call(
        _projconv_kernel,
        grid=(T // tm, N // tn),
        in_specs=[pl.BlockSpec((tm, K), lambda i, j: (i, 0)),
                  pl.BlockSpec((1, HALO, K), lambda i, j: (i, 0, 0)),
                  pl.BlockSpec((K, tn), lambda i, j: (0, j)),
                  pl.BlockSpec((3, tn), lambda i, j: (0, j))],
        out_specs=pl.BlockSpec((tm, tn), lambda i, j: (i, j)),
        out_shape=jax.ShapeDtypeStruct((T, N), jnp.float32),
        compiler_params=pltpu.CompilerParams(dimension_semantics=("arbitrary", "arbitrary"), vmem_limit_bytes=VMEM_LIMIT),
        name="proj_conv",
    )(xb.reshape(T, K), halo, w, ws)
    return out.reshape(B, L, N)


def _dft_mats(L):
    N = 2 * L
    f = lax.broadcasted_iota(jnp.int32, (L, L), 0)
    s = lax.broadcasted_iota(jnp.int32, (L, L), 1)
    ang = ((f * s) % N).astype(jnp.float32) * (2.0 * math.pi / N)
    a = jnp.cos(ang)
    sn = jnp.sin(ang)
    alt_s = (1 - 2 * (s % 2)).astype(jnp.float32)
    alt_f = (1 - 2 * (f % 2)).astype(jnp.float32)
    b = jnp.where(f == 0, alt_s, sn)
    b2 = jnp.where(s == 0, alt_f, sn)
    return a.astype(jnp.bfloat16), b.astype(jnp.bfloat16), b2.astype(jnp.bfloat16)


HY_BLOCK = 1024


def _hy_fwd_kernel(z_ref, a_ref, b_ref, kp_ref, kq_ref, yp_ref, yq_ref, zb_ref, *, tf, nb, P):
    j = pl.program_id(2)

    @pl.when(j == 0)
    def _():
        zb_ref[...] = z_ref[0].astype(jnp.bfloat16)

    a = a_ref[...]
    b = b_ref[...]
    ps = [jnp.dot(a, zb_ref[J * P:(J + 1) * P, :], preferred_element_type=jnp.float32) for J in range(nb)]
    qs = [jnp.dot(b, zb_ref[J * P:(J + 1) * P, :], preferred_element_type=jnp.float32) for J in range(nb)]
    row = lax.broadcasted_iota(jnp.int32, ps[0].shape, 0) + j * tf
    packed = row == 0
    for I in range(nb):
        yp = None
        yq = None
        for J in range(nb):
            kp = kp_ref[I - J + nb - 1]
            kq = kq_ref[I - J + nb - 1]
            kqq = kq * qs[J]
            tp = kp * ps[J] - jnp.where(packed, 0.0, kqq)
            tq = jnp.where(packed, kqq, kp * qs[J] + kq * ps[J])
            yp = tp if yp is None else yp + tp
            yq = tq if yq is None else yq + tq
        yp_ref[0, I] = yp.astype(yp_ref.dtype)
        yq_ref[0, I] = yq.astype(yq_ref.dtype)


def _hy_fwd(z, zcol, a, b, kp, kq, tf=256, tc=512):
    B, L, _ = z.shape
    nk, P, C = kp.shape
    nb = (nk + 1) // 2
    assert nb * P == L
    tf = min(tf, P)
    zspec = pl.BlockSpec((1, L, tc), lambda b_, c, j: (b_, 0, zcol + c))
    mspec = pl.BlockSpec((tf, P), lambda b_, c, j: (j, 0))
    kspec = pl.BlockSpec((nk, tf, tc), lambda b_, c, j: (0, j, c))
    ospec = pl.BlockSpec((1, nb, tf, tc), lambda b_, c, j: (b_, 0, j, c))
    return pl.pallas_call(
        functools.partial(_hy_fwd_kernel, tf=tf, nb=nb, P=P),
        grid=(B, C // tc, P // tf),
        in_specs=[zspec, mspec, mspec, kspec, kspec],
        out_specs=[ospec, ospec],
        out_shape=[jax.ShapeDtypeStruct((B, nb, P, C), jnp.bfloat16)] * 2,
        scratch_shapes=[pltpu.VMEM((L, tc), jnp.bfloat16)],
        compiler_params=pltpu.CompilerParams(dimension_semantics=("arbitrary",) * 3, vmem_limit_bytes=VMEM_LIMIT),
        name="hy_fwd",
    )(z, a, b, kp, kq)


def _hy_inv_kernel(yp_ref, yq_ref, a_ref, b2_ref, z_ref, x_ref, db_ref, o_ref):
    y = jnp.dot(a_ref[...], yp_ref[0, 0], preferred_element_type=jnp.float32)
    y += jnp.dot(b2_ref[...], yq_ref[0, 0], preferred_element_type=jnp.float32)
    o_ref[0] = (x_ref[0] * (y + z_ref[0] * db_ref[...])).astype(o_ref.dtype)


def _hy_inv(yp, yq, a, b2, zarr, zcol, xarr, xcol, db, out_dtype, tt=512, tc=512):
    B, nb, P, C = yp.shape
    tt = min(tt, P)
    nt = P // tt
    yspec = pl.BlockSpec((1, 1, P, tc), lambda b_, c, I, i: (b_, I, 0, c))
    mspec = pl.BlockSpec((tt, P), lambda b_, c, I, i: (i, 0))
    return pl.pallas_call(
        _hy_inv_kernel,
        grid=(B, C // tc, nb, nt),
        in_specs=[yspec, yspec, mspec, mspec,
                  pl.BlockSpec((1, tt, tc), lambda b_, c, I, i: (b_, I * nt + i, zcol + c)),
                  pl.BlockSpec((1, tt, tc), lambda b_, c, I, i: (b_, I * nt + i, xcol + c)),
                  pl.BlockSpec((1, tc), lambda b_, c, I, i: (0, c))],
        out_specs=pl.BlockSpec((1, tt, tc), lambda b_, c, I, i: (b_, I * nt + i, c)),
        out_shape=jax.ShapeDtypeStruct((B, nb * P, C), out_dtype),
        compiler_params=pltpu.CompilerParams(dimension_semantics=("arbitrary",) * 4, vmem_limit_bytes=VMEM_LIMIT),
        name="hy_inv",
    )(yp, yq, a, b2, zarr, xarr, db)


def _hy_spectra(filt, a, b, P, mm):
    L, n_ord, _, C = filt.shape
    nb = L // P
    N = 2 * P
    row0 = (jnp.arange(P) == 0)[:, None]
    zpad = jnp.zeros((P, C), filt.dtype)
    cols = []
    for o in range(n_ord):
        kext = jnp.concatenate([zpad, filt[:0:-1, o, 1], filt[:, o, 0], zpad, zpad[:1]], axis=0)
        for e in range(-(nb - 1), nb):
            i0 = e * P + L - 1 + P
            pos = kext[i0:i0 + P]
            neg = kext[i0 - P + 1:i0 + 1][::-1]
            cols += [pos, jnp.where(row0, 0.0, neg)]
    hcat = jnp.concatenate(cols, axis=1).astype(jnp.bfloat16)
    cp = mm(a, hcat)
    sp = mm(b, hcat)
    wgt = jnp.where(row0, 1.0 / N, 2.0 / N)
    out = []
    nk = 2 * nb - 1
    for o in range(n_ord):
        kps, kqs = [], []
        for k in range(nk):
            c0 = (o * nk + k) * 2 * C
            cf, cb = cp[:, c0:c0 + C], cp[:, c0 + C:c0 + 2 * C]
            sf, sb = sp[:, c0:c0 + C], sp[:, c0 + C:c0 + 2 * C]
            kps.append((cf + cb) * wgt)
            kqs.append(jnp.where(row0, sf + sb, sf - sb) * wgt)
        out.append((jnp.stack(kps), jnp.stack(kqs)))
    return out


def hyena(sc, filt, dbias, mm, P=HY_BLOCK, tc=512):
    L = sc.shape[1]
    C = filt.shape[3]
    P = min(P, L)
    a, b, b2 = _dft_mats(P)
    (kp1, kq1), (kp2, kq2) = _hy_spectra(filt, a, b, P, mm)
    db = dbias.astype(jnp.float32)
    nbc = C // tc
    yp, yq = _hy_fwd(sc, 0, a, b, kp1, kq1, tc=tc)
    z2 = _hy_inv(yp, yq, a, b2, sc, 0, sc, nbc, db[0:1], jnp.float32, tc=tc)
    yp, yq = _hy_fwd(z2, 0, a, b, kp2, kq2, tc=tc)
    return _hy_inv(yp, yq, a, b2, z2, 0, sc, 2 * nbc, db[1:2], jnp.bfloat16, tc=tc)


def _split_dot(x, ones_bf16):
    hi = x.astype(jnp.bfloat16)
    lo = (x - hi.astype(jnp.float32)).astype(jnp.bfloat16)
    return (jnp.dot(hi, ones_bf16, preferred_element_type=jnp.float32)
            + jnp.dot(lo, ones_bf16, preferred_element_type=jnp.float32))


def _head_ones():
    r = lax.broadcasted_iota(jnp.int32, (PAIR, PAIR), 0) // HEAD
    c = lax.broadcasted_iota(jnp.int32, (PAIR, PAIR), 1) // HEAD
    return (r == c).astype(jnp.bfloat16)


def _group_sum(x, ones):
    return jnp.concatenate([_split_dot(x[:, p * PAIR:(p + 1) * PAIR], ones) for p in range(x.shape[1] // PAIR)], axis=1)


def _softplus(x):
    return jnp.maximum(x, 0.0) + jnp.log1p(jnp.exp(-jnp.abs(x)))


def _wkv_fused_kernel(r_ref, k_ref, v_ref, lo_ref, w0_ref, a0_ref, w2_ref, a2_ref, kk_ref, ka_ref, o_ref, h_ref):
    rev = pl.program_id(0)

    @pl.when(pl.program_id(2) == 0)
    def _():
        h_ref[...] = jnp.zeros_like(h_ref)

    f32 = jnp.float32
    r = r_ref[0]
    k = k_ref[0]
    v = v_ref[0]
    xw = lo_ref[0, :, 0:PAIR]
    xa = lo_ref[0, :, PAIR:2 * PAIR]
    dw = jnp.dot(jnp.tanh(xw).astype(jnp.bfloat16), w2_ref[0], preferred_element_type=f32)
    lw = -jnp.exp(-_softplus(-(w0_ref[0] + dw)) - 0.5)
    a = jax.nn.sigmoid(a0_ref[0] + jnp.dot(xa.astype(jnp.bfloat16), a2_ref[0], preferred_element_type=f32))
    kq = k * kk_ref[...]
    ones = _head_ones()
    sq = (kq * kq).astype(jnp.bfloat16)
    ss = jnp.concatenate([jnp.dot(sq[:, p * PAIR:(p + 1) * PAIR], ones, preferred_element_type=f32)
                          for p in range(D_RW // PAIR)], axis=1)
    kk = kq / jnp.maximum(jnp.sqrt(ss), 1e-12)
    kdir = k * (1.0 + (a - 1.0) * ka_ref[...])
    kka = kk * a
    n = D_RW // PAIR
    sls = [slice(p * PAIR, (p + 1) * PAIR) for p in range(n)]
    ys, hn = _wkv_chunk([r[:, s] for s in sls], [lw[:, s] for s in sls], [kdir[:, s] for s in sls],
                        [v[:, s] for s in sls], [kk[:, s] for s in sls], [kka[:, s] for s in sls],
                        [h_ref[p] for p in range(n)], rev)
    for p in range(n):
        o_ref[0, 0, :, sls[p]] = ys[p]
        h_ref[p] = hn[p]


def wkv7_fused(sc, rcol, lcol, w0, a0, w2p, a2p, k_k, k_a):
    B, L, _ = sc.shape
    C = WKV_CHUNK
    nc = L // C
    D = D_RW
    assert L % C == 0
    cm = lambda d, c: c + d * (nc - 1 - 2 * c)
    col = lambda j: pl.BlockSpec((1, C, D), lambda d, b, c: (b, cm(d, c), j))
    vec2 = pl.BlockSpec((1, 1, D), lambda d, b, c: (d, 0, 0))
    mat2 = pl.BlockSpec((1, PAIR, D), lambda d, b, c: (d, 0, 0))
    vec = pl.BlockSpec((1, D), lambda d, b, c: (0, 0))
    return pl.pallas_call(
        _wkv_fused_kernel,
        grid=(2, B, nc),
        in_specs=[col(rcol), col(rcol + 1), col(rcol + 2),
                  pl.BlockSpec((1, C, LORA_W), lambda d, b, c: (b, cm(d, c), lcol)),
                  vec2, vec2, mat2, mat2, vec, vec],
        out_specs=pl.BlockSpec((1, 1, C, D), lambda d, b, c: (d, b, cm(d, c), 0)),
        out_shape=jax.ShapeDtypeStruct((2, B, L, D), jnp.float32),
        scratch_shapes=[pltpu.VMEM((D // PAIR, PAIR, PAIR), jnp.float32)],
        compiler_params=pltpu.CompilerParams(dimension_semantics=("arbitrary",) * 3, vmem_limit_bytes=VMEM_LIMIT),
        name="wkv7",
    )(sc, sc, sc, sc, w0, a0, w2p, a2p, k_k, k_a)


def _rw_post_kernel(y_ref, r_ref, k_ref, v_ref, lo_ref, a0_ref, a2_ref, ka_ref, rk_ref, lnw_ref, lnb_ref, g2_ref, wo_ref, o_ref):
    f32 = jnp.float32
    ones = _head_ones()
    y = y_ref[0, 0] + y_ref[1, 0]
    mu = _group_sum(y, ones) * (1.0 / HEAD)
    yc = y - mu
    var = _group_sum(yc * yc, ones) * (1.0 / HEAD)
    out = yc * lax.rsqrt(var + RW_GN_EPS) * lnw_ref[...] + lnb_ref[...]
    r = r_ref[0]
    k = k_ref[0]
    v = v_ref[0]
    xa = lo_ref[0, :, PAIR:2 * PAIR].astype(jnp.bfloat16)
    for d in range(2):
        a = jax.nn.sigmoid(a0_ref[d] + jnp.dot(xa, a2_ref[d], preferred_element_type=f32))
        kdir = k * (1.0 + (a - 1.0) * ka_ref[...])
        out = out + _group_sum(r * kdir * rk_ref[d], ones) * v
    g = jnp.dot(jax.nn.sigmoid(lo_ref[0]).astype(jnp.bfloat16), g2_ref[...], preferred_element_type=f32)
    o_ref[0] = jnp.dot((out * g).astype(jnp.bfloat16), wo_ref[...], preferred_element_type=f32).astype(o_ref.dtype)


def rw_post(y2, sc, rcol, lcol, a0, a2p, k_a, r_k, ln_w, ln_b, g2p, w_o, tm=256):
    _, B, L, D = y2.shape
    DM = w_o.shape[1]
    tm = min(tm, L)
    col = lambda j: pl.BlockSpec((1, tm, D), lambda b, i: (b, i, j))
    full = lambda shape: pl.BlockSpec(shape, lambda b, i: (0,) * len(shape))
    return pl.pallas_call(
        _rw_post_kernel,
        grid=(B, L // tm),
        in_specs=[pl.BlockSpec((2, 1, tm, D), lambda b, i: (0, b, i, 0)),
                  col(rcol), col(rcol + 1), col(rcol + 2),
                  pl.BlockSpec((1, tm, LORA_W), lambda b, i: (b, i, lcol)),
                  full((2, 1, D)), full((2, PAIR, D)), full((1, D)), full((2, 1, D)), full((1, D)), full((1, D)),
                  full((LORA_W, D)), full((D, DM))],
        out_specs=pl.BlockSpec((1, tm, DM), lambda b, i: (b, i, 0)),
        out_shape=jax.ShapeDtypeStruct((B, L, DM), jnp.bfloat16),
        compiler_params=pltpu.CompilerParams(dimension_semantics=("arbitrary",) * 2, vmem_limit_bytes=VMEM_LIMIT),
        name="rw_post",
    )(y2, sc, sc, sc, sc, a0, a2p, k_a, r_k, ln_w, ln_b, g2p, w_o)


def rwkv_params(p):
    f32, bf = jnp.float32, jnp.bfloat16
    z64 = jnp.zeros((64, D_RW), f32)
    w2p = jnp.stack([jnp.concatenate([p['rw_w2'][0], z64]), jnp.concatenate([z64, p['rw_w2'][1]])]).astype(bf)
    a2p = jnp.stack([jnp.concatenate([p['rw_a2'][0], z64]), jnp.concatenate([z64, p['rw_a2'][1]])]).astype(bf)
    g2p = jnp.zeros((LORA_W, D_RW), f32).at[256:256 + 160].set(p['rw_g2']).astype(bf)
    return dict(w0=p['rw_w0'].reshape(2, 1, D_RW), a0=p['rw_a0'].reshape(2, 1, D_RW), w2p=w2p, a2p=a2p, g2p=g2p,
                k_k=p['rw_k_k'].reshape(1, D_RW), k_a=p['rw_k_a'].reshape(1, D_RW), r_k=p['rw_r_k'].reshape(2, 1, D_RW),
                ln_w=p['rw_ln_w'].reshape(1, D_RW), ln_b=p['rw_ln_b'].reshape(1, D_RW), w_o=p['rw_w_o'].astype(bf))


def rwkv(sc, rcol, lcol, q):
    y2 = wkv7_fused(sc, rcol, lcol, q['w0'], q['a0'], q['w2p'], q['a2p'], q['k_k'], q['k_a'])
    return rw_post(y2, sc, rcol, lcol, q['a0'], q['a2p'], q['k_a'], q['r_k'], q['ln_w'], q['ln_b'], q['g2p'], q['w_o'])


def _attn_kernel(q_ref, k_ref, v_ref, wo_ref, o_ref):
    hp = pl.program_id(2)

    @pl.when(hp == 0)
    def _():
        o_ref[...] = jnp.zeros_like(o_ref)

    outs = []
    for h in range(CB // CA_HEAD):
        sl = slice(h * CA_HEAD, (h + 1) * CA_HEAD)
        q = q_ref[0, :, sl].astype(jnp.bfloat16)
        s = lax.dot_general(q, k_ref[0, :, sl].astype(jnp.bfloat16), (((1,), (1,)), ((), ())),
                            preferred_element_type=jnp.float32) * (CA_HEAD ** -0.5)
        s = s - jnp.max(s, axis=-1, keepdims=True)
        e = jnp.exp(s)
        p = e / jnp.sum(e, axis=-1, keepdims=True)
        outs.append(jnp.dot(p.astype(jnp.bfloat16), v_ref[0, :, sl].astype(jnp.bfloat16), preferred_element_type=jnp.float32))
    att = jnp.concatenate(outs, axis=1).astype(jnp.bfloat16)
    o_ref[0] += jnp.dot(att, wo_ref[...], preferred_element_type=jnp.float32)


def mem_attention(proj, qcol, kv, w_o, tm=512):
    B, L, _ = proj.shape
    M = kv.shape[1]
    DM = w_o.shape[1]
    nhp = w_o.shape[0] // CB
    tm = min(tm, L)
    return pl.pallas_call(
        _attn_kernel,
        grid=(B, L // tm, nhp),
        in_specs=[pl.BlockSpec((1, tm, CB), lambda b, i, h: (b, i, qcol + h)),
                  pl.BlockSpec((1, M, CB), lambda b, i, h: (b, 0, h)),
                  pl.BlockSpec((1, M, CB), lambda b, i, h: (b, 0, nhp + h)),
                  pl.BlockSpec((CB, DM), lambda b, i, h: (h, 0))],
        out_specs=pl.BlockSpec((1, tm, DM), lambda b, i, h: (b, i, 0)),
        out_shape=jax.ShapeDtypeStruct((B, L, DM), jnp.float32),
        compiler_params=pltpu.CompilerParams(dimension_semantics=("arbitrary",) * 3, vmem_limit_bytes=VMEM_LIMIT),
        name="mem_attn",
    )(proj, kv, kv, w_o)


def _merge_kernel(hy_ref, rw_ref, ca_ref, g0_ref, g1_ref, g2_ref, wo_ref, x_ref, lg_ref, lb_ref, wr_ref,
                  o_ref, ob_ref, lo_ref, acc_ref):
    kc = pl.program_id(2)

    @pl.when(kc == 0)
    def _():
        acc_ref[...] = jnp.zeros_like(acc_ref)

    f32 = jnp.float32
    m = (jax.nn.sigmoid(g0_ref[0].astype(f32)) * hy_ref[0] + jax.nn.sigmoid(g1_ref[0].astype(f32)) * rw_ref[0]
         + jax.nn.sigmoid(g2_ref[0].astype(f32)) * ca_ref[0])
    acc_ref[...] += jnp.dot(m.astype(jnp.bfloat16), wo_ref[...], preferred_element_type=jnp.float32)

    @pl.when(kc == pl.num_programs(2) - 1)
    def _():
        h = DN_ALPHA * x_ref[0] + acc_ref[...]
        mu = jnp.mean(h, axis=-1, keepdims=True)
        hc = h - mu
        var = jnp.mean(hc * hc, axis=-1, keepdims=True)
        y = hc * lax.rsqrt(var + LN_EPS) * lg_ref[...] + lb_ref[...]
        o_ref[0] = y
        yb = y.astype(jnp.bfloat16)
        ob_ref[0] = yb
        lo_ref[0] = jnp.dot(yb, wr_ref[...], preferred_element_type=jnp.float32)


def merge_ln(hy, rw, ca, proj, gcol, w_out, x, ln_g, ln_b, w_router_pad, tm=512):
    B, L, DM = x.shape
    nk = DM // CB
    tm = min(tm, L)
    br = pl.BlockSpec((1, tm, CB), lambda b, i, k: (b, i, k))
    gate = lambda g: pl.BlockSpec((1, tm, CB), lambda b, i, k: (b, i, gcol + g * nk + k))
    row = pl.BlockSpec((1, tm, DM), lambda b, i, k: (b, i, 0))
    vec = pl.BlockSpec((1, DM), lambda b, i, k: (0, 0))
    return pl.pallas_call(
        _merge_kernel,
        grid=(B, L // tm, nk),
        in_specs=[br, br, br, gate(0), gate(1), gate(2),
                  pl.BlockSpec((CB, DM), lambda b, i, k: (k, 0)), row, vec, vec,
                  pl.BlockSpec((DM, LANES), lambda b, i, k: (0, 0))],
        out_specs=[row, row, pl.BlockSpec((1, tm, LANES), lambda b, i, k: (b, i, 0))],
        out_shape=[jax.ShapeDtypeStruct((B, L, DM), jnp.float32), jax.ShapeDtypeStruct((B, L, DM), jnp.bfloat16),
                   jax.ShapeDtypeStruct((B, L, LANES), jnp.float32)],
        scratch_shapes=[pltpu.VMEM((tm, DM), jnp.float32)],
        compiler_params=pltpu.CompilerParams(dimension_semantics=("arbitrary",) * 3, vmem_limit_bytes=VMEM_LIMIT),
        name="merge_ln",
    )(hy, rw, ca, proj, proj, proj, w_out, x, ln_g, ln_b, w_router_pad)


def _ln_kernel(x_ref, m_ref, g_ref, b_ref, o_ref):
    h = DN_ALPHA * x_ref[...] + m_ref[...]
    mu = jnp.mean(h, axis=-1, keepdims=True)
    hc = h - mu
    var = jnp.mean(hc * hc, axis=-1, keepdims=True)
    o_ref[...] = hc * lax.rsqrt(var + LN_EPS) * g_ref[...] + b_ref[...]


def resid_ln(x, m, g, b, tm=512):
    T, DM = x.shape
    row = pl.BlockSpec((tm, DM), lambda i: (i, 0))
    vec = pl.BlockSpec((1, DM), lambda i: (0, 0))
    return pl.pallas_call(
        _ln_kernel, grid=(T // tm,), in_specs=[row, row, vec, vec], out_specs=row,
        out_shape=jax.ShapeDtypeStruct((T, DM), jnp.float32),
        compiler_params=pltpu.CompilerParams(dimension_semantics=("arbitrary",), vmem_limit_bytes=VMEM_LIMIT),
        name="resid_ln",
    )(x, m, g, b)


def _hyena_filters(L, w1, b1, w2, b2, w3, b3, w4, sin_freq):
    f32 = jnp.float32
    t = jnp.linspace(0.0, 1.0, L, dtype=f32)[:, None]
    ang = 2.0 * math.pi * jnp.arange(L, dtype=f32)[:, None] / L
    bands = jnp.linspace(1e-4, HY_BANDS - 1, HY_BANDS, dtype=f32)[None]
    z = jnp.concatenate([t, jnp.cos(bands * ang), -jnp.sin(bands * ang)], axis=-1)
    sf = sin_freq.astype(f32)
    h = jnp.sin(sf[0] * (z @ w1.astype(f32) + b1.astype(f32)))
    h = jnp.sin(sf[1] * (h @ w2.astype(f32) + b2.astype(f32)))
    h = jnp.sin(sf[2] * (h @ w3.astype(f32) + b3.astype(f32)))
    h = (h @ w4.astype(f32)).reshape(L, HY_ORDER, 2, D_HY)
    deltas = jnp.abs(jnp.linspace(math.log(HY_DECAY_TARGET) / HY_SLOW_DECAY_PCT,
                                  math.log(HY_DECAY_TARGET) / HY_FAST_DECAY_PCT, D_HY, dtype=f32))
    window = jnp.exp(-t * deltas[None])
    return h * window[:, None, None, :]


def _expert_choice_moe(xt, xb, logits, w_gate, w_up, w_down):
    T, D = xt.shape
    cap = (EC_CAPACITY * T) // N_EXPERTS
    aff = jax.nn.softmax(logits.astype(jnp.float32), axis=-1)
    gval, idx = lax.top_k(aff.T, cap)
    out = _moe_ffn(xb[idx], jnp.broadcast_to(gval[..., None], gval.shape + (LANES,)), w_gate, w_up, w_down)
    return jnp.zeros_like(xt).at[idx.reshape(-1)].add(out.reshape(-1, D))


def _layer(x, mem, p):
    B, L, D = x.shape
    xb = x.astype(jnp.bfloat16)
    sc = proj_conv(xb, p['w_conv'], p['w_short_all'])
    pq = _mm3(xb, p['w_rest'], out_dtype=jnp.bfloat16)
    filt = _hyena_filters(L, p['hy_w1'], p['hy_b1'], p['hy_w2'], p['hy_b2'], p['hy_w3'], p['hy_b3'],
                          p['hy_w4'], p['hy_sin_freq'])
    hy = _mm3(hyena(sc, filt, p['hy_dbias'], _matmul), p['hy_w_o'], out_dtype=jnp.bfloat16)
    rw = rwkv(sc, P_RW // D_RW, P_LORA // LORA_W, p['rw'])
    ca = mem_attention(pq, 0, _mm3(mem, p['ca_w_kv']), p['ca_w_o'])
    x1, x1b, logits = merge_ln(hy, rw, ca, pq, (P_GATE - N_CONV) // CB, p['w_out'], x, p['ln1_g'], p['ln1_b'], p['w_router_pad'])
    moe = _expert_choice_moe(x1.reshape(B * L, D), x1b.reshape(B * L, D), logits.reshape(B * L, LANES)[:, :N_EXPERTS],
                             p['moe_w_gate'], p['moe_w_up'], p['moe_w_down'])
    return resid_ln(x1.reshape(B * L, D), moe, p['ln2_g'], p['ln2_b']).reshape(B, L, D)


def kernel(x_prompt, x_sample, mem_prompt, mem_sample, w_in, w_short, hy_w1, hy_b1, hy_w2, hy_b2, hy_w3, hy_b3, hy_w4, hy_sin_freq, hy_dbias, hy_w_o, rw_w0, rw_w2, rw_a0, rw_a2, rw_g2, rw_k_k, rw_k_a, rw_r_k, rw_ln_w, rw_ln_b, rw_w_o, ca_w_kv, ca_w_o, w_out, ln1_g, ln1_b, moe_w_router, moe_w_gate, moe_w_up, moe_w_down, ln2_g, ln2_b):
    params = dict(w_in=w_in, w_short=w_short, hy_w1=hy_w1, hy_b1=hy_b1, hy_w2=hy_w2, hy_b2=hy_b2,
                  hy_w3=hy_w3, hy_b3=hy_b3, hy_w4=hy_w4, hy_sin_freq=hy_sin_freq, hy_dbias=hy_dbias,
                  hy_w_o=hy_w_o, rw_w0=rw_w0, rw_w2=rw_w2, rw_a0=rw_a0, rw_a2=rw_a2, rw_g2=rw_g2,
                  rw_k_k=rw_k_k, rw_k_a=rw_k_a, rw_r_k=rw_r_k, rw_ln_w=rw_ln_w, rw_ln_b=rw_ln_b,
                  rw_w_o=rw_w_o, ca_w_kv=ca_w_kv, ca_w_o=ca_w_o, w_out=w_out, ln1_g=ln1_g, ln1_b=ln1_b,
                  moe_w_router=moe_w_router, moe_w_gate=moe_w_gate, moe_w_up=moe_w_up,
                  moe_w_down=moe_w_down, ln2_g=ln2_g, ln2_b=ln2_b)
    bf = jnp.bfloat16
    y_prompt, y_sample = x_prompt, x_sample
    for i in range(DEPTH):
        lp = {name: arr[i] for name, arr in params.items()}
        wi = lp['w_in']
        lp['w_conv'] = jnp.pad(wi[:, :OFF_Q], ((0, 0), (0, N_LORA_PAD - N_LORA))).astype(bf)
        lp['w_rest'] = wi[:, OFF_Q:].astype(bf)
        lp['w_short_all'] = jnp.pad(lp['w_short'], ((0, 0), (0, N_LORA_PAD - N_LORA)))
        for name in ('hy_w_o', 'ca_w_o', 'ca_w_kv', 'w_out'):
            lp[name] = lp[name].astype(bf)
        lp['rw'] = rwkv_params(lp)
        lp['w_router_pad'] = jnp.pad(lp['moe_w_router'], ((0, 0), (0, LANES - N_EXPERTS))).astype(bf)
        for name in ('ln1_g', 'ln1_b', 'ln2_g', 'ln2_b'):
            lp[name] = lp[name].reshape(1, D_MODEL)
        y_prompt = _layer(y_prompt, mem_prompt, lp)
        y_sample = _layer(y_sample, mem_sample, lp)
    return (y_prompt, y_sample)
```

```python
import functools
import math

import jax
import jax.numpy as jnp
from jax import lax
from jax.experimental import pallas as pl
from jax.experimental.pallas import tpu as pltpu

D_MODEL = 2048
DEPTH = 1
D_HY = D_MODEL // 2
HY_ORDER = 2
HY_EMB = 33
HY_BANDS = (HY_EMB - 1) // 2
HY_FFN = 64
HY_FAST_DECAY_PCT = 0.3
HY_SLOW_DECAY_PCT = 1.5
HY_DECAY_TARGET = 1e-2
D_RW = D_MODEL // 2
RW_HEAD = 64
RW_HEADS = D_RW // RW_HEAD
RW_DECAY_LORA = 64
RW_AAA_LORA = 64
RW_GATE_LORA = 160
RW_GN_EPS = 64e-5
D_CA = D_MODEL // 2
CA_HEADS = 4
CA_HEAD = D_CA // CA_HEADS
N_BRANCH = 3
N_EXPERTS = 16
EC_CAPACITY = 2
D_EXPERT = 2 * D_MODEL
LN_EPS = 1e-5
DN_ALPHA = (2 * DEPTH) ** 0.25

OFF_HY = 0
OFF_RW = OFF_HY + 3 * D_HY
OFF_LORA = OFF_RW + 3 * D_RW
N_LORA = 2 * RW_DECAY_LORA + 2 * RW_AAA_LORA + RW_GATE_LORA
OFF_Q = OFF_LORA + N_LORA
OFF_GATE = OFF_Q + D_CA
N_IN = OFF_GATE + N_BRANCH * D_MODEL

LANES = 128
N_LORA_PAD = -(-N_LORA // LANES) * LANES
P_HY = 0
P_RW = P_HY + 3 * D_HY
P_LORA = P_RW + 3 * D_RW
P_Q = P_LORA + N_LORA_PAD
P_GATE = P_Q + D_CA
P_ALL = P_GATE + N_BRANCH * D_MODEL

VMEM_LIMIT = 56 * 1024 * 1024
SUBLANES = 8
CB = 512
N_CONV = P_Q
LORA_W = N_LORA_PAD


def _mm_kernel(a_ref, b_ref, o_ref):
    o_ref[...] = jnp.dot(a_ref[...].astype(jnp.bfloat16), b_ref[...].astype(jnp.bfloat16),
                         preferred_element_type=jnp.float32).astype(o_ref.dtype)


def _matmul(a, b, tm=1024, tn=512, out_dtype=jnp.float32):
    M, K = a.shape
    N = b.shape[1]
    tm = min(tm, M)
    tn = min(tn, N)
    assert M % tm == 0 and N % tn == 0, (M, N, tm, tn)
    return pl.pallas_call(
        _mm_kernel,
        grid=(M // tm, N // tn),
        in_specs=[pl.BlockSpec((tm, K), lambda i, j: (i, 0)),
                  pl.BlockSpec((K, tn), lambda i, j: (0, j))],
        out_specs=pl.BlockSpec((tm, tn), lambda i, j: (i, j)),
        out_shape=jax.ShapeDtypeStruct((M, N), out_dtype),
        compiler_params=pltpu.CompilerParams(dimension_semantics=("arbitrary", "arbitrary"),
                                             vmem_limit_bytes=VMEM_LIMIT),
        name="mm",
    )(a, b)


def _mm3(x, w, **kw):
    B, L, K = x.shape
    return _matmul(x.reshape(B * L, K), w, **kw).reshape(B, L, w.shape[1])


def _moe_kernel(x_ref, gv_ref, wg_ref, wu_ref, wd_ref, o_ref):
    @pl.when(pl.program_id(2) == 0)
    def _():
        o_ref[...] = jnp.zeros_like(o_ref)

    x = x_ref[0]
    hg = jnp.dot(x, wg_ref[0].astype(jnp.bfloat16), preferred_element_type=jnp.float32)
    hu = jnp.dot(x, wu_ref[0].astype(jnp.bfloat16), preferred_element_type=jnp.float32)
    h = (hg * jax.nn.sigmoid(hg) * hu).astype(jnp.bfloat16)
    o_ref[0] += jnp.dot(h, wd_ref[0].astype(jnp.bfloat16), preferred_element_type=jnp.float32)

    @pl.when(pl.program_id(2) == pl.num_programs(2) - 1)
    def _():
        o_ref[0] = o_ref[0] * jnp.tile(gv_ref[0], (1, o_ref.shape[2] // LANES))


def _moe_ffn(xe, gv, w_gate, w_up, w_down, tm=1024, tf=256):
    E, C, D = xe.shape
    F = w_gate.shape[2]
    tm = min(tm, C)
    return pl.pallas_call(
        _moe_kernel,
        grid=(E, C // tm, F // tf),
        in_specs=[pl.BlockSpec((1, tm, D), lambda e, i, f: (e, i, 0)),
                  pl.BlockSpec((1, tm, LANES), lambda e, i, f: (e, i, 0)),
                  pl.BlockSpec((1, D, tf), lambda e, i, f: (e, 0, f)),
                  pl.BlockSpec((1, D, tf), lambda e, i, f: (e, 0, f)),
                  pl.BlockSpec((1, tf, D), lambda e, i, f: (e, f, 0))],
        out_specs=pl.BlockSpec((1, tm, D), lambda e, i, f: (e, i, 0)),
        out_shape=jax.ShapeDtypeStruct((E, C, D), jnp.float32),
        compiler_params=pltpu.CompilerParams(dimension_semantics=("arbitrary",) * 3, vmem_limit_bytes=VMEM_LIMIT),
        name="moe_ffn",
    )(xe, gv, w_gate, w_up, w_down)


WKV_CHUNK = 64
HEAD = RW_HEAD
PAIR = 2 * HEAD


def _bdot(a, b):
    return jnp.dot(a.astype(jnp.bfloat16), b.astype(jnp.bfloat16), preferred_element_type=jnp.float32)


def _bdot_nt(a, b):
    return lax.dot_general(a.astype(jnp.bfloat16), b.astype(jnp.bfloat16), (((1,), (1,)), ((), ())),
                           preferred_element_type=jnp.float32)


def _bdot_tn(a, b):
    return lax.dot_general(a.astype(jnp.bfloat16), b.astype(jnp.bfloat16), (((0,), (0,)), ((), ())),
                           preferred_element_type=jnp.float32)


def _split3(x):
    f32, bf = jnp.float32, jnp.bfloat16
    hi = x.astype(bf)
    r1 = x - hi.astype(f32)
    mid = r1.astype(bf)
    lo = (r1 - mid.astype(f32)).astype(bf)
    return hi, mid, lo


def _wkv_chunk(rs, lws, ks, vs, kks, kkas, ss, rev):
    C = WKV_CHUNK
    f32, bf = jnp.float32, jnp.bfloat16
    n = len(rs)
    ri = lax.broadcasted_iota(jnp.int32, (C, C), 0)
    ci = lax.broadcasted_iota(jnp.int32, (C, C), 1)
    dd = jnp.where(rev == 0, ri - ci, ci - ri)
    tri = (dd >= 0).astype(bf)
    lane = lax.broadcasted_iota(jnp.int32, (C, PAIR), 1)
    first = lane < HEAD
    r2 = lax.broadcasted_iota(jnp.int32, (2 * C, 2 * C), 0)
    c2 = lax.broadcasted_iota(jnp.int32, (2 * C, 2 * C), 1)
    d2 = jnp.where(rev == 0, r2 - c2, c2 - r2)
    strict = d2 > 0
    incl = d2 >= 0
    eye = (r2 == c2).astype(f32)

    def stack(x):
        return jnp.concatenate([jnp.where(first, x, 0.0), jnp.where(first, 0.0, x)], axis=0)

    cums = []
    for lw in lws:
        hi, mid, lo = _split3(lw)
        cums.append(jnp.dot(tri, hi, preferred_element_type=f32) + jnp.dot(tri, mid, preferred_element_type=f32)
                    + jnp.dot(tri, lo, preferred_element_type=f32))
    galls = [jnp.exp(jnp.sum(lw, axis=0, keepdims=True)) for lw in lws]
    lhs, rhs, vS, a2k2 = [], [], [], []
    for i in range(n):
        cum, lw = cums[i], lws[i]
        ginv = jnp.exp(-cum)
        bt = stack(kks[i] * jnp.exp(cum - lw))
        rt = stack(rs[i] * jnp.exp(cum))
        at = stack(-(kkas[i] * ginv))
        kt = stack(ks[i] * ginv)
        lhs.append(jnp.concatenate([bt, rt], axis=0).astype(bf))
        rhs.append(jnp.concatenate([at, kt], axis=0).astype(bf))
        a2k2.append((jnp.concatenate([at, kt], axis=0) * galls[i]).astype(bf))
        vS.append(stack(vs[i]).astype(bf))
    gram = [_bdot_nt(lhs[i], rhs[i]) for i in range(n)]
    C2 = 2 * C
    mab = [jnp.where(strict, g[:C2, :C2], 0.0) for g in gram]
    mbk = [jnp.where(strict, g[:C2, C2:], 0.0).astype(bf) for g in gram]
    mrr = [jnp.concatenate([jnp.where(incl, g[C2:, :C2], 0.0), jnp.where(incl, g[C2:, C2:], 0.0)], axis=1).astype(bf)
           for g in gram]
    ts = [eye + m for m in mab]
    ps = [_bdot(m, m) for m in mab]
    for _ in range(4):
        xs = [_bdot(jnp.concatenate([t, p], axis=0), p) for t, p in zip(ts, ps)]
        ts = [t + x[:C2] for t, x in zip(ts, xs)]
        ps = [x[C2:] for x in xs]
    ts = [t + _bdot(t, p) for t, p in zip(ts, ps)]
    sB = [s.astype(bf) for s in ss]
    hs = [_bdot_nt(lhs[i], sB[i]) for i in range(n)]
    wS = [hs[i][:C2] + _bdot(mbk[i], vS[i]) for i in range(n)]
    uS = [_bdot(ts[i], wS[i]).astype(bf) for i in range(n)]
    uv = [jnp.concatenate([uS[i], vS[i]], axis=0) for i in range(n)]
    yS = [hs[i][C2:] + _bdot(mrr[i], uv[i]) for i in range(n)]
    ys = [y[:C] + y[C:] for y in yS]
    sn = [ss[i] * galls[i] + _bdot_tn(uv[i], a2k2[i]) for i in range(n)]
    return ys, sn


HALO = 2 * SUBLANES


def _projconv_kernel(x_ref, h_ref, w_ref, ws_ref, o_ref):
    w = w_ref[...]
    p = jnp.dot(x_ref[...], w, preferred_element_type=jnp.float32)
    ph = jnp.dot(h_ref[0], w, preferred_element_type=jnp.float32)
    tm = p.shape[0]
    rid = lax.broadcasted_iota(jnp.int32, (tm, 1), 0)
    prev = jnp.where(rid == 0, ph[SUBLANES - 1:SUBLANES, :], pltpu.roll(p, 1, 0))
    nxt = jnp.where(rid == tm - 1, ph[SUBLANES:SUBLANES + 1, :], pltpu.roll(p, tm - 1, 0))
    o_ref[...] = ws_ref[0:1, :] * prev + ws_ref[1:2, :] * p + ws_ref[2:3, :] * nxt


def _halo_rows(xb, tm):
    B, L, K = xb.shape
    nt = L // tm
    xr = xb.reshape(B, nt, tm, K)
    zero = jnp.zeros((B, 1, SUBLANES, K), xb.dtype)
    above = jnp.concatenate([zero, xr[:, :-1, tm - SUBLANES:]], axis=1)
    below = jnp.concatenate([xr[:, 1:, :SUBLANES], zero], axis=1)
    return jnp.concatenate([above, below], axis=2).reshape(B * nt, HALO, K)


def proj_conv(xb, w, ws, tm=1024, tn=512):
    B, L, K = xb.shape
    N = w.shape[1]
    tm = min(tm, L)
    assert L % tm == 0 and N % tn == 0
    halo = _halo_rows(xb, tm)
    T = B * L
    out = pl.pallas_call(
        _projconv_kernel,
        grid=(T // tm, N // tn),
        in_specs=[pl.BlockSpec((tm, K), lambda i, j: (i, 0)),
                  pl.BlockSpec((1, HALO, K), lambda i, j: (i, 0, 0)),
                  pl.BlockSpec((K, tn), lambda i, j: (0, j)),
                  pl.BlockSpec((3, tn), lambda i, j: (0, j))],
        out_specs=pl.BlockSpec((tm, tn), lambda i, j: (i, j)),
        out_shape=jax.ShapeDtypeStruct((T, N), jnp.float32),
        compiler_params=pltpu.CompilerParams(dimension_semantics=("arbitrary", "arbitrary"), vmem_limit_bytes=VMEM_LIMIT),
        name="proj_conv",
    )(xb.reshape(T, K), halo, w, ws)
    return out.reshape(B, L, N)


def _dft_mats(L):
    N = 2 * L
    f = lax.broadcasted_iota(jnp.int32, (L, L), 0)
    s = lax.broadcasted_iota(jnp.int32, (L, L), 1)
    ang = ((f * s) % N).astype(jnp.float32) * (2.0 * math.pi / N)
    a = jnp.cos(ang)
    sn = jnp.sin(ang)
    alt_s = (1 - 2 * (s % 2)).astype(jnp.float32)
    alt_f = (1 - 2 * (f % 2)).astype(jnp.float32)
    b = jnp.where(f == 0, alt_s, sn)
    b2 = jnp.where(s == 0, alt_f, sn)
    return a.astype(jnp.bfloat16), b.astype(jnp.bfloat16), b2.astype(jnp.bfloat16)


HY_BLOCK = 1024


def _hy_fwd_kernel(z_ref, a_ref, b_ref, kp_ref, kq_ref, yp_ref, yq_ref, zb_ref, *, tf, nb, P):
    j = pl.program_id(2)

    @pl.when(j == 0)
    def _():
        zb_ref[...] = z_ref[0].astype(jnp.bfloat16)

    a = a_ref[...]
    b = b_ref[...]
    ps = [jnp.dot(a, zb_ref[J * P:(J + 1) * P, :], preferred_element_type=jnp.float32) for J in range(nb)]
    qs = [jnp.dot(b, zb_ref[J * P:(J + 1) * P, :], preferred_element_type=jnp.float32) for J in range(nb)]
    row = lax.broadcasted_iota(jnp.int32, ps[0].shape, 0) + j * tf
    packed = row == 0
    for I in range(nb):
        yp = None
        yq = None
        for J in range(nb):
            kp = kp_ref[I - J + nb - 1]
            kq = kq_ref[I - J + nb - 1]
            kqq = kq * qs[J]
            tp = kp * ps[J] - jnp.where(packed, 0.0, kqq)
            tq = jnp.where(packed, kqq, kp * qs[J] + kq * ps[J])
            yp = tp if yp is None else yp + tp
            yq = tq if yq is None else yq + tq
        yp_ref[0, I] = yp.astype(yp_ref.dtype)
        yq_ref[0, I] = yq.astype(yq_ref.dtype)


def _hy_fwd(z, zcol, a, b, kp, kq, tf=256, tc=512):
    B, L, _ = z.shape
    nk, P, C = kp.shape
    nb = (nk + 1) // 2
    assert nb * P == L
    tf = min(tf, P)
    zspec = pl.BlockSpec((1, L, tc), lambda b_, c, j: (b_, 0, zcol + c))
    mspec = pl.BlockSpec((tf, P), lambda b_, c, j: (j, 0))
    kspec = pl.BlockSpec((nk, tf, tc), lambda b_, c, j: (0, j, c))
    ospec = pl.BlockSpec((1, nb, tf, tc), lambda b_, c, j: (b_, 0, j, c))
    return pl.pallas_call(
        functools.partial(_hy_fwd_kernel, tf=tf, nb=nb, P=P),
        grid=(B, C // tc, P // tf),
        in_specs=[zspec, mspec, mspec, kspec, kspec],
        out_specs=[ospec, ospec],
        out_shape=[jax.ShapeDtypeStruct((B, nb, P, C), jnp.bfloat16)] * 2,
        scratch_shapes=[pltpu.VMEM((L, tc), jnp.bfloat16)],
        compiler_params=pltpu.CompilerParams(dimension_semantics=("arbitrary",) * 3, vmem_limit_bytes=VMEM_LIMIT),
        name="hy_fwd",
    )(z, a, b, kp, kq)


def _hy_inv_kernel(yp_ref, yq_ref, a_ref, b2_ref, z_ref, x_ref, db_ref, o_ref):
    y = jnp.dot(a_ref[...], yp_ref[0, 0], preferred_element_type=jnp.float32)
    y += jnp.dot(b2_ref[...], yq_ref[0, 0], preferred_element_type=jnp.float32)
    o_ref[0] = (x_ref[0] * (y + z_ref[0] * db_ref[...])).astype(o_ref.dtype)


def _hy_inv(yp, yq, a, b2, zarr, zcol, xarr, xcol, db, out_dtype, tt=512, tc=512):
    B, nb, P, C = yp.shape
    tt = min(tt, P)
    nt = P // tt
    yspec = pl.BlockSpec((1, 1, P, tc), lambda b_, c, I, i: (b_, I, 0, c))
    mspec = pl.BlockSpec((tt, P), lambda b_, c, I, i: (i, 0))
    return pl.pallas_call(
        _hy_inv_kernel,
        grid=(B, C // tc, nb, nt),
        in_specs=[yspec, yspec, mspec, mspec,
                  pl.BlockSpec((1, tt, tc), lambda b_, c, I, i: (b_, I * nt + i, zcol + c)),
                  pl.BlockSpec((1, tt, tc), lambda b_, c, I, i: (b_, I * nt + i, xcol + c)),
                  pl.BlockSpec((1, tc), lambda b_, c, I, i: (0, c))],
        out_specs=pl.BlockSpec((1, tt, tc), lambda b_, c, I, i: (b_, I * nt + i, c)),
        out_shape=jax.ShapeDtypeStruct((B, nb * P, C), out_dtype),
        compiler_params=pltpu.CompilerParams(dimension_semantics=("arbitrary",) * 4, vmem_limit_bytes=VMEM_LIMIT),
        name="hy_inv",
    )(yp, yq, a, b2, zarr, xarr, db)


def _hy_spectra(filt, a, b, P, mm):
    L, n_ord, _, C = filt.shape
    nb = L // P
    N = 2 * P
    assert P % 2 == 0
    hcat = jnp.concatenate([filt[:, o, s].reshape(nb, P, C) for o in range(n_ord) for s in range(2)], axis=0)
    ns = hcat.shape[0]
    flat = jnp.transpose(hcat, (1, 0, 2)).reshape(P, ns * C).astype(jnp.bfloat16)
    cs = mm(a, flat).reshape(P, ns, C)
    sn = mm(b, flat).reshape(P, ns, C)
    row0 = (jnp.arange(P) == 0)[:, None]
    sign = (1 - 2 * (jnp.arange(P) % 2)).astype(jnp.float32)[:, None]
    wgt = jnp.where(row0, 1.0 / N, 2.0 / N)
    zero = jnp.zeros((P, C), jnp.float32)
    out = []
    for o in range(n_ord):
        def fwd(s, j, drop_first):
            if j >= nb:
                return zero, zero
            k = (o * 2 + s) * nb + j
            x0 = hcat[k, 0:1, :].astype(jnp.bfloat16).astype(jnp.float32) if drop_first else 0.0
            return cs[:, k] - x0, sn[:, k] - jnp.where(row0, x0, 0.0)

        def bwd(s, j):
            c, q = fwd(s, j, True)
            return sign * c, jnp.where(row0, q, -sign * q)

        def first(s, j):
            if j >= nb:
                return zero, zero
            k = (o * 2 + s) * nb + j
            x0 = jnp.broadcast_to(hcat[k, 0:1, :].astype(jnp.bfloat16).astype(jnp.float32), (P, C))
            return x0, jnp.where(row0, x0, 0.0)

        kps, kqs = [], []
        for e in range(-(nb - 1), nb):
            if e >= 1:
                cp, qp = fwd(0, e, False)
                cn, qn = bwd(0, e - 1)
            elif e == 0:
                cp, qp = fwd(0, 0, False)
                cn, qn = fwd(1, 0, True)
            else:
                g = -e
                c1, q1 = bwd(1, g - 1)
                c2, q2 = first(1, g)
                cp, qp = c1 + c2, q1 + q2
                cn, qn = fwd(1, g, True)
            kps.append((cp + cn) * wgt)
            kqs.append(jnp.where(row0, qp + qn, qp - qn) * wgt)
        out.append((jnp.stack(kps), jnp.stack(kqs)))
    return out


def hyena(sc, filt, dbias, mm, P=HY_BLOCK, tc=512):
    L = sc.shape[1]
    C = filt.shape[3]
    P = min(P, L)
    a, b, b2 = _dft_mats(P)
    (kp1, kq1), (kp2, kq2) = _hy_spectra(filt, a, b, P, mm)
    db = dbias.astype(jnp.float32)
    nbc = C // tc
    yp, yq = _hy_fwd(sc, 0, a, b, kp1, kq1, tc=tc)
    z2 = _hy_inv(yp, yq, a, b2, sc, 0, sc, nbc, db[0:1], jnp.float32, tc=tc)
    yp, yq = _hy_fwd(z2, 0, a, b, kp2, kq2, tc=tc)
    return _hy_inv(yp, yq, a, b2, z2, 0, sc, 2 * nbc, db[1:2], jnp.bfloat16, tc=tc)


def _split_dot(x, ones_bf16):
    hi = x.astype(jnp.bfloat16)
    lo = (x - hi.astype(jnp.float32)).astype(jnp.bfloat16)
    return (jnp.dot(hi, ones_bf16, preferred_element_type=jnp.float32)
            + jnp.dot(lo, ones_bf16, preferred_element_type=jnp.float32))


def _head_ones():
    r = lax.broadcasted_iota(jnp.int32, (PAIR, PAIR), 0) // HEAD
    c = lax.broadcasted_iota(jnp.int32, (PAIR, PAIR), 1) // HEAD
    return (r == c).astype(jnp.bfloat16)


def _group_sum(x, ones):
    return jnp.concatenate([_split_dot(x[:, p * PAIR:(p + 1) * PAIR], ones) for p in range(x.shape[1] // PAIR)], axis=1)


def _softplus(x):
    return jnp.maximum(x, 0.0) + jnp.log1p(jnp.exp(-jnp.abs(x)))


def _wkv_fused_kernel(r_ref, k_ref, v_ref, lo_ref, w0_ref, a0_ref, w2_ref, a2_ref, kk_ref, ka_ref, o_ref, h_ref):
    rev = pl.program_id(0)

    @pl.when(pl.program_id(2) == 0)
    def _():
        h_ref[...] = jnp.zeros_like(h_ref)

    f32 = jnp.float32
    r = r_ref[0]
    k = k_ref[0]
    v = v_ref[0]
    xw = lo_ref[0, :, 0:PAIR]
    xa = lo_ref[0, :, PAIR:2 * PAIR]
    dw = jnp.dot(jnp.tanh(xw).astype(jnp.bfloat16), w2_ref[0], preferred_element_type=f32)
    lw = -jnp.exp(-_softplus(-(w0_ref[0] + dw)) - 0.5)
    a = jax.nn.sigmoid(a0_ref[0] + jnp.dot(xa.astype(jnp.bfloat16), a2_ref[0], preferred_element_type=f32))
    kq = k * kk_ref[...]
    ones = _head_ones()
    sq = (kq * kq).astype(jnp.bfloat16)
    ss = jnp.concatenate([jnp.dot(sq[:, p * PAIR:(p + 1) * PAIR], ones, preferred_element_type=f32)
                          for p in range(D_RW // PAIR)], axis=1)
    kk = kq / jnp.maximum(jnp.sqrt(ss), 1e-12)
    kdir = k * (1.0 + (a - 1.0) * ka_ref[...])
    kka = kk * a
    n = D_RW // PAIR
    sls = [slice(p * PAIR, (p + 1) * PAIR) for p in range(n)]
    ys, hn = _wkv_chunk([r[:, s] for s in sls], [lw[:, s] for s in sls], [kdir[:, s] for s in sls],
                        [v[:, s] for s in sls], [kk[:, s] for s in sls], [kka[:, s] for s in sls],
                        [h_ref[p] for p in range(n)], rev)
    for p in range(n):
        o_ref[0, 0, :, sls[p]] = ys[p]
        h_ref[p] = hn[p]


def wkv7_fused(sc, rcol, lcol, w0, a0, w2p, a2p, k_k, k_a):
    B, L, _ = sc.shape
    C = WKV_CHUNK
    nc = L // C
    D = D_RW
    assert L % C == 0
    cm = lambda d, c: c + d * (nc - 1 - 2 * c)
    col = lambda j: pl.BlockSpec((1, C, D), lambda d, b, c: (b, cm(d, c), j))
    vec2 = pl.BlockSpec((1, 1, D), lambda d, b, c: (d, 0, 0))
    mat2 = pl.BlockSpec((1, PAIR, D), lambda d, b, c: (d, 0, 0))
    vec = pl.BlockSpec((1, D), lambda d, b, c: (0, 0))
    return pl.pallas_call(
        _wkv_fused_kernel,
        grid=(2, B, nc),
        in_specs=[col(rcol), col(rcol + 1), col(rcol + 2),
                  pl.BlockSpec((1, C, LORA_W), lambda d, b, c: (b, cm(d, c), lcol)),
                  vec2, vec2, mat2, mat2, vec, vec],
        out_specs=pl.BlockSpec((1, 1, C, D), lambda d, b, c: (d, b, cm(d, c), 0)),
        out_shape=jax.ShapeDtypeStruct((2, B, L, D), jnp.float32),
        scratch_shapes=[pltpu.VMEM((D // PAIR, PAIR, PAIR), jnp.float32)],
        compiler_params=pltpu.CompilerParams(dimension_semantics=("arbitrary",) * 3, vmem_limit_bytes=VMEM_LIMIT),
        name="wkv7",
    )(sc, sc, sc, sc, w0, a0, w2p, a2p, k_k, k_a)


def _rw_post_kernel(y_ref, r_ref, k_ref, v_ref, lo_ref, a0_ref, a2_ref, ka_ref, rk_ref, lnw_ref, lnb_ref, g2_ref, wo_ref, o_ref):
    f32 = jnp.float32
    ones = _head_ones()
    y = y_ref[0, 0] + y_ref[1, 0]
    mu = _group_sum(y, ones) * (1.0 / HEAD)
    yc = y - mu
    var = _group_sum(yc * yc, ones) * (1.0 / HEAD)
    out = yc * lax.rsqrt(var + RW_GN_EPS) * lnw_ref[...] + lnb_ref[...]
    r = r_ref[0]
    k = k_ref[0]
    v = v_ref[0]
    xa = lo_ref[0, :, PAIR:2 * PAIR].astype(jnp.bfloat16)
    for d in range(2):
        a = jax.nn.sigmoid(a0_ref[d] + jnp.dot(xa, a2_ref[d], preferred_element_type=f32))
        kdir = k * (1.0 + (a - 1.0) * ka_ref[...])
        out = out + _group_sum(r * kdir * rk_ref[d], ones) * v
    g = jnp.dot(jax.nn.sigmoid(lo_ref[0]).astype(jnp.bfloat16), g2_ref[...], preferred_element_type=f32)
    o_ref[0] = jnp.dot((out * g).astype(jnp.bfloat16), wo_ref[...], preferred_element_type=f32).astype(o_ref.dtype)


def rw_post(y2, sc, rcol, lcol, a0, a2p, k_a, r_k, ln_w, ln_b, g2p, w_o, tm=256):
    _, B, L, D = y2.shape
    DM = w_o.shape[1]
    tm = min(tm, L)
    col = lambda j: pl.BlockSpec((1, tm, D), lambda b, i: (b, i, j))
    full = lambda shape: pl.BlockSpec(shape, lambda b, i: (0,) * len(shape))
    return pl.pallas_call(
        _rw_post_kernel,
        grid=(B, L // tm),
        in_specs=[pl.BlockSpec((2, 1, tm, D), lambda b, i: (0, b, i, 0)),
                  col(rcol), col(rcol + 1), col(rcol + 2),
                  pl.BlockSpec((1, tm, LORA_W), lambda b, i: (b, i, lcol)),
                  full((2, 1, D)), full((2, PAIR, D)), full((1, D)), full((2, 1, D)), full((1, D)), full((1, D)),
                  full((LORA_W, D)), full((D, DM))],
        out_specs=pl.BlockSpec((1, tm, DM), lambda b, i: (b, i, 0)),
        out_shape=jax.ShapeDtypeStruct((B, L, DM), jnp.bfloat16),
        compiler_params=pltpu.CompilerParams(dimension_semantics=("arbitrary",) * 2, vmem_limit_bytes=VMEM_LIMIT),
        name="rw_post",
    )(y2, sc, sc, sc, sc, a0, a2p, k_a, r_k, ln_w, ln_b, g2p, w_o)


def rwkv_params(p):
    f32, bf = jnp.float32, jnp.bfloat16
    z64 = jnp.zeros((64, D_RW), f32)
    w2p = jnp.stack([jnp.concatenate([p['rw_w2'][0], z64]), jnp.concatenate([z64, p['rw_w2'][1]])]).astype(bf)
    a2p = jnp.stack([jnp.concatenate([p['rw_a2'][0], z64]), jnp.concatenate([z64, p['rw_a2'][1]])]).astype(bf)
    g2p = jnp.zeros((LORA_W, D_RW), f32).at[256:256 + 160].set(p['rw_g2']).astype(bf)
    return dict(w0=p['rw_w0'].reshape(2, 1, D_RW), a0=p['rw_a0'].reshape(2, 1, D_RW), w2p=w2p, a2p=a2p, g2p=g2p,
                k_k=p['rw_k_k'].reshape(1, D_RW), k_a=p['rw_k_a'].reshape(1, D_RW), r_k=p['rw_r_k'].reshape(2, 1, D_RW),
                ln_w=p['rw_ln_w'].reshape(1, D_RW), ln_b=p['rw_ln_b'].reshape(1, D_RW), w_o=p['rw_w_o'].astype(bf))


def rwkv(sc, rcol, lcol, q):
    y2 = wkv7_fused(sc, rcol, lcol, q['w0'], q['a0'], q['w2p'], q['a2p'], q['k_k'], q['k_a'])
    return rw_post(y2, sc, rcol, lcol, q['a0'], q['a2p'], q['k_a'], q['r_k'], q['ln_w'], q['ln_b'], q['g2p'], q['w_o'])


def _attn_kernel(q_ref, k_ref, v_ref, wo_ref, o_ref):
    hp = pl.program_id(2)

    @pl.when(hp == 0)
    def _():
        o_ref[...] = jnp.zeros_like(o_ref)

    outs = []
    for h in range(CB // CA_HEAD):
        sl = slice(h * CA_HEAD, (h + 1) * CA_HEAD)
        q = q_ref[0, :, sl].astype(jnp.bfloat16)
        s = lax.dot_general(q, k_ref[0, :, sl].astype(jnp.bfloat16), (((1,), (1,)), ((), ())),
                            preferred_element_type=jnp.float32) * (CA_HEAD ** -0.5)
        s = s - jnp.max(s, axis=-1, keepdims=True)
        e = jnp.exp(s)
        p = e / jnp.sum(e, axis=-1, keepdims=True)
        outs.append(jnp.dot(p.astype(jnp.bfloat16), v_ref[0, :, sl].astype(jnp.bfloat16), preferred_element_type=jnp.float32))
    att = jnp.concatenate(outs, axis=1).astype(jnp.bfloat16)
    o_ref[0] += jnp.dot(att, wo_ref[...], preferred_element_type=jnp.float32)


def mem_attention(proj, qcol, kv, w_o, tm=512):
    B, L, _ = proj.shape
    M = kv.shape[1]
    DM = w_o.shape[1]
    nhp = w_o.shape[0] // CB
    tm = min(tm, L)
    return pl.pallas_call(
        _attn_kernel,
        grid=(B, L // tm, nhp),
        in_specs=[pl.BlockSpec((1, tm, CB), lambda b, i, h: (b, i, qcol + h)),
                  pl.BlockSpec((1, M, CB), lambda b, i, h: (b, 0, h)),
                  pl.BlockSpec((1, M, CB), lambda b, i, h: (b, 0, nhp + h)),
                  pl.BlockSpec((CB, DM), lambda b, i, h: (h, 0))],
        out_specs=pl.BlockSpec((1, tm, DM), lambda b, i, h: (b, i, 0)),
        out_shape=jax.ShapeDtypeStruct((B, L, DM), jnp.float32),
        compiler_params=pltpu.CompilerParams(dimension_semantics=("arbitrary",) * 3, vmem_limit_bytes=VMEM_LIMIT),
        name="mem_attn",
    )(proj, kv, kv, w_o)


def _merge_kernel(hy_ref, rw_ref, ca_ref, g0_ref, g1_ref, g2_ref, wo_ref, x_ref, lg_ref, lb_ref, wr_ref,
                  o_ref, ob_ref, lo_ref):
    f32 = jnp.float32
    m = (jax.nn.sigmoid(g0_ref[0].astype(f32)) * hy_ref[0] + jax.nn.sigmoid(g1_ref[0].astype(f32)) * rw_ref[0]
         + jax.nn.sigmoid(g2_ref[0].astype(f32)) * ca_ref[0])
    h = DN_ALPHA * x_ref[0] + jnp.dot(m.astype(jnp.bfloat16), wo_ref[...], preferred_element_type=f32)
    mu = jnp.mean(h, axis=-1, keepdims=True)
    hc = h - mu
    var = jnp.mean(hc * hc, axis=-1, keepdims=True)
    y = hc * lax.rsqrt(var + LN_EPS) * lg_ref[...] + lb_ref[...]
    o_ref[0] = y
    yb = y.astype(jnp.bfloat16)
    ob_ref[0] = yb
    lo_ref[0] = jnp.dot(yb, wr_ref[...], preferred_element_type=f32)


def merge_ln(hy, rw, ca, pq, gcol, w_out, x, ln_g, ln_b, w_router_pad, tm=256):
    B, L, DM = x.shape
    tm = min(tm, L)
    row = pl.BlockSpec((1, tm, DM), lambda b, i: (b, i, 0))
    gate = lambda g: pl.BlockSpec((1, tm, DM), lambda b, i: (b, i, gcol + g))
    vec = pl.BlockSpec((1, DM), lambda b, i: (0, 0))
    return pl.pallas_call(
        _merge_kernel,
        grid=(B, L // tm),
        in_specs=[row, row, row, gate(0), gate(1), gate(2),
                  pl.BlockSpec((DM, DM), lambda b, i: (0, 0)), row, vec, vec,
                  pl.BlockSpec((DM, LANES), lambda b, i: (0, 0))],
        out_specs=[row, row, pl.BlockSpec((1, tm, LANES), lambda b, i: (b, i, 0))],
        out_shape=[jax.ShapeDtypeStruct((B, L, DM), jnp.float32), jax.ShapeDtypeStruct((B, L, DM), jnp.bfloat16),
                   jax.ShapeDtypeStruct((B, L, LANES), jnp.float32)],
        compiler_params=pltpu.CompilerParams(dimension_semantics=("arbitrary",) * 2, vmem_limit_bytes=VMEM_LIMIT),
        name="merge_ln",
    )(hy, rw, ca, pq, pq, pq, w_out, x, ln_g, ln_b, w_router_pad)


def _ln_kernel(x_ref, m_ref, g_ref, b_ref, o_ref):
    h = DN_ALPHA * x_ref[...] + m_ref[...]
    mu = jnp.mean(h, axis=-1, keepdims=True)
    hc = h - mu
    var = jnp.mean(hc * hc, axis=-1, keepdims=True)
    o_ref[...] = hc * lax.rsqrt(var + LN_EPS) * g_ref[...] + b_ref[...]


def resid_ln(x, m, g, b, tm=512):
    T, DM = x.shape
    row = pl.BlockSpec((tm, DM), lambda i: (i, 0))
    vec = pl.BlockSpec((1, DM), lambda i: (0, 0))
    return pl.pallas_call(
        _ln_kernel, grid=(T // tm,), in_specs=[row, row, vec, vec], out_specs=row,
        out_shape=jax.ShapeDtypeStruct((T, DM), jnp.float32),
        compiler_params=pltpu.CompilerParams(dimension_semantics=("arbitrary",), vmem_limit_bytes=VMEM_LIMIT),
        name="resid_ln",
    )(x, m, g, b)


def _hyena_filters(L, w1, b1, w2, b2, w3, b3, w4, sin_freq):
    f32 = jnp.float32
    t = jnp.linspace(0.0, 1.0, L, dtype=f32)[:, None]
    ang = 2.0 * math.pi * jnp.arange(L, dtype=f32)[:, None] / L
    bands = jnp.linspace(1e-4, HY_BANDS - 1, HY_BANDS, dtype=f32)[None]
    z = jnp.concatenate([t, jnp.cos(bands * ang), -jnp.sin(bands * ang)], axis=-1)
    sf = sin_freq.astype(f32)
    h = jnp.sin(sf[0] * (z @ w1.astype(f32) + b1.astype(f32)))
    h = jnp.sin(sf[1] * (h @ w2.astype(f32) + b2.astype(f32)))
    h = jnp.sin(sf[2] * (h @ w3.astype(f32) + b3.astype(f32)))
    h = (h @ w4.astype(f32)).reshape(L, HY_ORDER, 2, D_HY)
    deltas = jnp.abs(jnp.linspace(math.log(HY_DECAY_TARGET) / HY_SLOW_DECAY_PCT,
                                  math.log(HY_DECAY_TARGET) / HY_FAST_DECAY_PCT, D_HY, dtype=f32))
    window = jnp.exp(-t * deltas[None])
    return h * window[:, None, None, :]


def _expert_choice_moe(xt, xb, logits, w_gate, w_up, w_down):
    T, D = xt.shape
    cap = (EC_CAPACITY * T) // N_EXPERTS
    aff = jax.nn.softmax(logits.astype(jnp.float32), axis=-1)
    gval, idx = lax.top_k(aff.T, cap)
    out = _moe_ffn(xb[idx], jnp.broadcast_to(gval[..., None], gval.shape + (LANES,)), w_gate, w_up, w_down)
    return jnp.zeros_like(xt).at[idx.reshape(-1)].add(out.reshape(-1, D))


def _layer(x, mem, p):
    B, L, D = x.shape
    xb = x.astype(jnp.bfloat16)
    sc = proj_conv(xb, p['w_conv'], p['w_short_all'])
    pq = _mm3(xb, p['w_rest'], out_dtype=jnp.bfloat16)
    filt = _hyena_filters(L, p['hy_w1'], p['hy_b1'], p['hy_w2'], p['hy_b2'], p['hy_w3'], p['hy_b3'],
                          p['hy_w4'], p['hy_sin_freq'])
    hy = _mm3(hyena(sc, filt, p['hy_dbias'], _matmul), p['hy_w_o'], out_dtype=jnp.bfloat16)
    rw = rwkv(sc, P_RW // D_RW, P_LORA // LORA_W, p['rw'])
    ca = mem_attention(pq, N_BRANCH * D_MODEL // CB, _mm3(mem, p['ca_w_kv']), p['ca_w_o'])
    x1, x1b, logits = merge_ln(hy, rw, ca, pq, 0, p['w_out'], x, p['ln1_g'], p['ln1_b'], p['w_router_pad'])
    moe = _expert_choice_moe(x1.reshape(B * L, D), x1b.reshape(B * L, D), logits.reshape(B * L, LANES)[:, :N_EXPERTS],
                             p['moe_w_gate'], p['moe_w_up'], p['moe_w_down'])
    return resid_ln(x1.reshape(B * L, D), moe, p['ln2_g'], p['ln2_b']).reshape(B, L, D)


def kernel(x_prompt, x_sample, mem_prompt, mem_sample, w_in, w_short, hy_w1, hy_b1, hy_w2, hy_b2, hy_w3, hy_b3, hy_w4, hy_sin_freq, hy_dbias, hy_w_o, rw_w0, rw_w2, rw_a0, rw_a2, rw_g2, rw_k_k, rw_k_a, rw_r_k, rw_ln_w, rw_ln_b, rw_w_o, ca_w_kv, ca_w_o, w_out, ln1_g, ln1_b, moe_w_router, moe_w_gate, moe_w_up, moe_w_down, ln2_g, ln2_b):
    params = dict(w_in=w_in, w_short=w_short, hy_w1=hy_w1, hy_b1=hy_b1, hy_w2=hy_w2, hy_b2=hy_b2,
                  hy_w3=hy_w3, hy_b3=hy_b3, hy_w4=hy_w4, hy_sin_freq=hy_sin_freq, hy_dbias=hy_dbias,
                  hy_w_o=hy_w_o, rw_w0=rw_w0, rw_w2=rw_w2, rw_a0=rw_a0, rw_a2=rw_a2, rw_g2=rw_g2,
                  rw_k_k=rw_k_k, rw_k_a=rw_k_a, rw_r_k=rw_r_k, rw_ln_w=rw_ln_w, rw_ln_b=rw_ln_b,
                  rw_w_o=rw_w_o, ca_w_kv=ca_w_kv, ca_w_o=ca_w_o, w_out=w_out, ln1_g=ln1_g, ln1_b=ln1_b,
                  moe_w_router=moe_w_router, moe_w_gate=moe_w_gate, moe_w_up=moe_w_up,
                  moe_w_down=moe_w_down, ln2_g=ln2_g, ln2_b=ln2_b)
    bf = jnp.bfloat16
    y_prompt, y_sample = x_prompt, x_sample
    for i in range(DEPTH):
        lp = {name: arr[i] for name, arr in params.items()}
        wi = lp['w_in']
        lp['w_conv'] = jnp.pad(wi[:, :OFF_Q], ((0, 0), (0, N_LORA_PAD - N_LORA))).astype(bf)
        lp['w_rest'] = jnp.concatenate([wi[:, OFF_GATE:], wi[:, OFF_Q:OFF_GATE]], axis=1).astype(bf)
        lp['w_short_all'] = jnp.pad(lp['w_short'], ((0, 0), (0, N_LORA_PAD - N_LORA)))
        for name in ('hy_w_o', 'ca_w_o', 'ca_w_kv', 'w_out'):
            lp[name] = lp[name].astype(bf)
        lp['rw'] = rwkv_params(lp)
        lp['w_router_pad'] = jnp.pad(lp['moe_w_router'], ((0, 0), (0, LANES - N_EXPERTS))).astype(bf)
        for name in ('ln1_g', 'ln1_b', 'ln2_g', 'ln2_b'):
            lp[name] = lp[name].reshape(1, D_MODEL)
        y_prompt = _layer(y_prompt, mem_prompt, lp)
        y_sample = _layer(y_sample, mem_sample, lp)
    return (y_prompt, y_sample)
```

```python
import functools
import math

import jax
import jax.numpy as jnp
from jax import lax
from jax.experimental import pallas as pl
from jax.experimental.pallas import tpu as pltpu

D_MODEL = 2048
DEPTH = 1
D_HY = D_MODEL // 2
HY_ORDER = 2
HY_EMB = 33
HY_BANDS = (HY_EMB - 1) // 2
HY_FFN = 64
HY_FAST_DECAY_PCT = 0.3
HY_SLOW_DECAY_PCT = 1.5
HY_DECAY_TARGET = 1e-2
D_RW = D_MODEL // 2
RW_HEAD = 64
RW_HEADS = D_RW // RW_HEAD
RW_DECAY_LORA = 64
RW_AAA_LORA = 64
RW_GATE_LORA = 160
RW_GN_EPS = 64e-5
D_CA = D_MODEL // 2
CA_HEADS = 4
CA_HEAD = D_CA // CA_HEADS
N_BRANCH = 3
N_EXPERTS = 16
EC_CAPACITY = 2
D_EXPERT = 2 * D_MODEL
LN_EPS = 1e-5
DN_ALPHA = (2 * DEPTH) ** 0.25

OFF_HY = 0
OFF_RW = OFF_HY + 3 * D_HY
OFF_LORA = OFF_RW + 3 * D_RW
N_LORA = 2 * RW_DECAY_LORA + 2 * RW_AAA_LORA + RW_GATE_LORA
OFF_Q = OFF_LORA + N_LORA
OFF_GATE = OFF_Q + D_CA
N_IN = OFF_GATE + N_BRANCH * D_MODEL

LANES = 128
N_LORA_PAD = -(-N_LORA // LANES) * LANES
P_HY = 0
P_RW = P_HY + 3 * D_HY
P_LORA = P_RW + 3 * D_RW
P_Q = P_LORA + N_LORA_PAD
P_GATE = P_Q + D_CA
P_ALL = P_GATE + N_BRANCH * D_MODEL

VMEM_LIMIT = 56 * 1024 * 1024
SUBLANES = 8
CB = 512
N_CONV = P_Q
LORA_W = N_LORA_PAD


def _mm_kernel(a_ref, b_ref, o_ref):
    o_ref[...] = jnp.dot(a_ref[...].astype(jnp.bfloat16), b_ref[...].astype(jnp.bfloat16),
                         preferred_element_type=jnp.float32).astype(o_ref.dtype)


def _matmul(a, b, tm=1024, tn=512, out_dtype=jnp.float32):
    M, K = a.shape
    N = b.shape[1]
    tm = min(tm, M)
    tn = min(tn, N)
    assert M % tm == 0 and N % tn == 0, (M, N, tm, tn)
    return pl.pallas_call(
        _mm_kernel,
        grid=(M // tm, N // tn),
        in_specs=[pl.BlockSpec((tm, K), lambda i, j: (i, 0)),
                  pl.BlockSpec((K, tn), lambda i, j: (0, j))],
        out_specs=pl.BlockSpec((tm, tn), lambda i, j: (i, j)),
        out_shape=jax.ShapeDtypeStruct((M, N), out_dtype),
        compiler_params=pltpu.CompilerParams(dimension_semantics=("arbitrary", "arbitrary"),
                                             vmem_limit_bytes=VMEM_LIMIT),
        name="mm",
    )(a, b)


def _mm3(x, w, **kw):
    B, L, K = x.shape
    return _matmul(x.reshape(B * L, K), w, **kw).reshape(B, L, w.shape[1])


def _moe_kernel(x_ref, gv_ref, wg_ref, wu_ref, wd_ref, o_ref):
    @pl.when(pl.program_id(2) == 0)
    def _():
        o_ref[...] = jnp.zeros_like(o_ref)

    x = x_ref[0]
    hg = jnp.dot(x, wg_ref[0].astype(jnp.bfloat16), preferred_element_type=jnp.float32)
    hu = jnp.dot(x, wu_ref[0].astype(jnp.bfloat16), preferred_element_type=jnp.float32)
    h = (hg * jax.nn.sigmoid(hg) * hu).astype(jnp.bfloat16)
    o_ref[0] += jnp.dot(h, wd_ref[0].astype(jnp.bfloat16), preferred_element_type=jnp.float32)

    @pl.when(pl.program_id(2) == pl.num_programs(2) - 1)
    def _():
        o_ref[0] = o_ref[0] * jnp.tile(gv_ref[0], (1, o_ref.shape[2] // LANES))


def _moe_ffn(xe, gv, w_gate, w_up, w_down, tm=1024, tf=256):
    E, C, D = xe.shape
    F = w_gate.shape[2]
    tm = min(tm, C)
    return pl.pallas_call(
        _moe_kernel,
        grid=(E, C // tm, F // tf),
        in_specs=[pl.BlockSpec((1, tm, D), lambda e, i, f: (e, i, 0)),
                  pl.BlockSpec((1, tm, LANES), lambda e, i, f: (e, i, 0)),
                  pl.BlockSpec((1, D, tf), lambda e, i, f: (e, 0, f)),
                  pl.BlockSpec((1, D, tf), lambda e, i, f: (e, 0, f)),
                  pl.BlockSpec((1, tf, D), lambda e, i, f: (e, f, 0))],
        out_specs=pl.BlockSpec((1, tm, D), lambda e, i, f: (e, i, 0)),
        out_shape=jax.ShapeDtypeStruct((E, C, D), jnp.float32),
        compiler_params=pltpu.CompilerParams(dimension_semantics=("arbitrary",) * 3, vmem_limit_bytes=VMEM_LIMIT),
        name="moe_ffn",
    )(xe, gv, w_gate, w_up, w_down)


WKV_CHUNK = 64
HEAD = RW_HEAD
PAIR = 2 * HEAD


def _bdot(a, b):
    return jnp.dot(a.astype(jnp.bfloat16), b.astype(jnp.bfloat16), preferred_element_type=jnp.float32)


def _bdot_nt(a, b):
    return lax.dot_general(a.astype(jnp.bfloat16), b.astype(jnp.bfloat16), (((1,), (1,)), ((), ())),
                           preferred_element_type=jnp.float32)


def _bdot_tn(a, b):
    return lax.dot_general(a.astype(jnp.bfloat16), b.astype(jnp.bfloat16), (((0,), (0,)), ((), ())),
                           preferred_element_type=jnp.float32)


def _split3(x):
    f32, bf = jnp.float32, jnp.bfloat16
    hi = x.astype(bf)
    r1 = x - hi.astype(f32)
    mid = r1.astype(bf)
    lo = (r1 - mid.astype(f32)).astype(bf)
    return hi, mid, lo


def _wkv_chunk(rs, lws, ks, vs, kks, kkas, ss, rev):
    C = WKV_CHUNK
    f32, bf = jnp.float32, jnp.bfloat16
    n = len(rs)
    ri = lax.broadcasted_iota(jnp.int32, (C, C), 0)
    ci = lax.broadcasted_iota(jnp.int32, (C, C), 1)
    dd = jnp.where(rev == 0, ri - ci, ci - ri)
    tri = (dd >= 0).astype(bf)
    lane = lax.broadcasted_iota(jnp.int32, (C, PAIR), 1)
    first = lane < HEAD
    r2 = lax.broadcasted_iota(jnp.int32, (2 * C, 2 * C), 0)
    c2 = lax.broadcasted_iota(jnp.int32, (2 * C, 2 * C), 1)
    d2 = jnp.where(rev == 0, r2 - c2, c2 - r2)
    strict = d2 > 0
    incl = d2 >= 0
    eye = (r2 == c2).astype(f32)

    def stack(x):
        return jnp.concatenate([jnp.where(first, x, 0.0), jnp.where(first, 0.0, x)], axis=0)

    cums = []
    for lw in lws:
        hi, mid, lo = _split3(lw)
        cums.append(jnp.dot(tri, hi, preferred_element_type=f32) + jnp.dot(tri, mid, preferred_element_type=f32)
                    + jnp.dot(tri, lo, preferred_element_type=f32))
    galls = [jnp.exp(jnp.sum(lw, axis=0, keepdims=True)) for lw in lws]
    lhs, rhs, vS, a2k2 = [], [], [], []
    for i in range(n):
        cum, lw = cums[i], lws[i]
        ginv = jnp.exp(-cum)
        bt = stack(kks[i] * jnp.exp(cum - lw))
        rt = stack(rs[i] * jnp.exp(cum))
        at = stack(-(kkas[i] * ginv))
        kt = stack(ks[i] * ginv)
        lhs.append(jnp.concatenate([bt, rt], axis=0).astype(bf))
        rhs.append(jnp.concatenate([at, kt], axis=0).astype(bf))
        a2k2.append((jnp.concatenate([at, kt], axis=0) * galls[i]).astype(bf))
        vS.append(stack(vs[i]).astype(bf))
    gram = [_bdot_nt(lhs[i], rhs[i]) for i in range(n)]
    C2 = 2 * C
    mab = [jnp.where(strict, g[:C2, :C2], 0.0) for g in gram]
    mbk = [jnp.where(strict, g[:C2, C2:], 0.0).astype(bf) for g in gram]
    mrr = [jnp.concatenate([jnp.where(incl, g[C2:, :C2], 0.0), jnp.where(incl, g[C2:, C2:], 0.0)], axis=1).astype(bf)
           for g in gram]
    ts = [eye + m for m in mab]
    ps = [_bdot(m, m) for m in mab]
    for _ in range(4):
        xs = [_bdot(jnp.concatenate([t, p], axis=0), p) for t, p in zip(ts, ps)]
        ts = [t + x[:C2] for t, x in zip(ts, xs)]
        ps = [x[C2:] for x in xs]
    ts = [t + _bdot(t, p) for t, p in zip(ts, ps)]
    sB = [s.astype(bf) for s in ss]
    hs = [_bdot_nt(lhs[i], sB[i]) for i in range(n)]
    wS = [hs[i][:C2] + _bdot(mbk[i], vS[i]) for i in range(n)]
    uS = [_bdot(ts[i], wS[i]).astype(bf) for i in range(n)]
    uv = [jnp.concatenate([uS[i], vS[i]], axis=0) for i in range(n)]
    yS = [hs[i][C2:] + _bdot(mrr[i], uv[i]) for i in range(n)]
    ys = [y[:C] + y[C:] for y in yS]
    sn = [ss[i] * galls[i] + _bdot_tn(uv[i], a2k2[i]) for i in range(n)]
    return ys, sn


HALO = 2 * SUBLANES


def _projconv_kernel(x_ref, h_ref, w_ref, ws_ref, o_ref):
    w = w_ref[...]
    p = jnp.dot(x_ref[...], w, preferred_element_type=jnp.float32)
    ph = jnp.dot(h_ref[0], w, preferred_element_type=jnp.float32)
    tm = p.shape[0]
    rid = lax.broadcasted_iota(jnp.int32, (tm, 1), 0)
    prev = jnp.where(rid == 0, ph[SUBLANES - 1:SUBLANES, :], pltpu.roll(p, 1, 0))
    nxt = jnp.where(rid == tm - 1, ph[SUBLANES:SUBLANES + 1, :], pltpu.roll(p, tm - 1, 0))
    o_ref[...] = ws_ref[0:1, :] * prev + ws_ref[1:2, :] * p + ws_ref[2:3, :] * nxt


def _halo_rows(xb, tm):
    B, L, K = xb.shape
    nt = L // tm
    xr = xb.reshape(B, nt, tm, K)
    zero = jnp.zeros((B, 1, SUBLANES, K), xb.dtype)
    above = jnp.concatenate([zero, xr[:, :-1, tm - SUBLANES:]], axis=1)
    below = jnp.concatenate([xr[:, 1:, :SUBLANES], zero], axis=1)
    return jnp.concatenate([above, below], axis=2).reshape(B * nt, HALO, K)


def proj_conv(xb, w, ws, tm=1024, tn=512):
    B, L, K = xb.shape
    N = w.shape[1]
    tm = min(tm, L)
    assert L % tm == 0 and N % tn == 0
    halo = _halo_rows(xb, tm)
    T = B * L
    out = pl.pallas_call(
        _projconv_kernel,
        grid=(T // tm, N // tn),
        in_specs=[pl.BlockSpec((tm, K), lambda i, j: (i, 0)),
                  pl.BlockSpec((1, HALO, K), lambda i, j: (i, 0, 0)),
                  pl.BlockSpec((K, tn), lambda i, j: (0, j)),
                  pl.BlockSpec((3, tn), lambda i, j: (0, j))],
        out_specs=pl.BlockSpec((tm, tn), lambda i, j: (i, j)),
        out_shape=jax.ShapeDtypeStruct((T, N), jnp.float32),
        compiler_params=pltpu.CompilerParams(dimension_semantics=("arbitrary", "arbitrary"), vmem_limit_bytes=VMEM_LIMIT),
        name="proj_conv",
    )(xb.reshape(T, K), halo, w, ws)
    return out.reshape(B, L, N)


def _dft_mats(L):
    N = 2 * L
    f = lax.broadcasted_iota(jnp.int32, (L, L), 0)
    s = lax.broadcasted_iota(jnp.int32, (L, L), 1)
    ang = ((f * s) % N).astype(jnp.float32) * (2.0 * math.pi / N)
    a = jnp.cos(ang)
    sn = jnp.sin(ang)
    alt_s = (1 - 2 * (s % 2)).astype(jnp.float32)
    alt_f = (1 - 2 * (f % 2)).astype(jnp.float32)
    b = jnp.where(f == 0, alt_s, sn)
    b2 = jnp.where(s == 0, alt_f, sn)
    return a.astype(jnp.bfloat16), b.astype(jnp.bfloat16), b2.astype(jnp.bfloat16)


HY_BLOCK = 1024


def _hy_fwd_kernel(z_ref, a_ref, b_ref, kp_ref, kq_ref, yp_ref, yq_ref, zb_ref, *, tf, nb, P):
    j = pl.program_id(2)

    @pl.when(j == 0)
    def _():
        zb_ref[...] = z_ref[0].astype(jnp.bfloat16)

    a = a_ref[...]
    b = b_ref[...]
    ps = [jnp.dot(a, zb_ref[J * P:(J + 1) * P, :], preferred_element_type=jnp.float32) for J in range(nb)]
    qs = [jnp.dot(b, zb_ref[J * P:(J + 1) * P, :], preferred_element_type=jnp.float32) for J in range(nb)]
    row = lax.broadcasted_iota(jnp.int32, ps[0].shape, 0) + j * tf
    packed = row == 0
    for I in range(nb):
        yp = None
        yq = None
        for J in range(nb):
            kp = kp_ref[I - J + nb - 1]
            kq = kq_ref[I - J + nb - 1]
            kqq = kq * qs[J]
            tp = kp * ps[J] - jnp.where(packed, 0.0, kqq)
            tq = jnp.where(packed, kqq, kp * qs[J] + kq * ps[J])
            yp = tp if yp is None else yp + tp
            yq = tq if yq is None else yq + tq
        yp_ref[0, I] = yp.astype(yp_ref.dtype)
        yq_ref[0, I] = yq.astype(yq_ref.dtype)


def _hy_fwd(z, zcol, a, b, kp, kq, tf=256, tc=512):
    B, L, _ = z.shape
    nk, P, C = kp.shape
    nb = (nk + 1) // 2
    assert nb * P == L
    tf = min(tf, P)
    zspec = pl.BlockSpec((1, L, tc), lambda b_, c, j: (b_, 0, zcol + c))
    mspec = pl.BlockSpec((tf, P), lambda b_, c, j: (j, 0))
    kspec = pl.BlockSpec((nk, tf, tc), lambda b_, c, j: (0, j, c))
    ospec = pl.BlockSpec((1, nb, tf, tc), lambda b_, c, j: (b_, 0, j, c))
    return pl.pallas_call(
        functools.partial(_hy_fwd_kernel, tf=tf, nb=nb, P=P),
        grid=(B, C // tc, P // tf),
        in_specs=[zspec, mspec, mspec, kspec, kspec],
        out_specs=[ospec, ospec],
        out_shape=[jax.ShapeDtypeStruct((B, nb, P, C), jnp.bfloat16)] * 2,
        scratch_shapes=[pltpu.VMEM((L, tc), jnp.bfloat16)],
        compiler_params=pltpu.CompilerParams(dimension_semantics=("arbitrary",) * 3, vmem_limit_bytes=VMEM_LIMIT),
        name="hy_fwd",
    )(z, a, b, kp, kq)


def _hy_inv_kernel(yp_ref, yq_ref, a_ref, b2_ref, z_ref, x_ref, db_ref, o_ref):
    y = jnp.dot(a_ref[...], yp_ref[0, 0], preferred_element_type=jnp.float32)
    y += jnp.dot(b2_ref[...], yq_ref[0, 0], preferred_element_type=jnp.float32)
    o_ref[0] = (x_ref[0] * (y + z_ref[0] * db_ref[...])).astype(o_ref.dtype)


def _hy_inv(yp, yq, a, b2, zarr, zcol, xarr, xcol, db, out_dtype, tt=512, tc=512):
    B, nb, P, C = yp.shape
    tt = min(tt, P)
    nt = P // tt
    yspec = pl.BlockSpec((1, 1, P, tc), lambda b_, c, I, i: (b_, I, 0, c))
    mspec = pl.BlockSpec((tt, P), lambda b_, c, I, i: (i, 0))
    return pl.pallas_call(
        _hy_inv_kernel,
        grid=(B, C // tc, nb, nt),
        in_specs=[yspec, yspec, mspec, mspec,
                  pl.BlockSpec((1, tt, tc), lambda b_, c, I, i: (b_, I * nt + i, zcol + c)),
                  pl.BlockSpec((1, tt, tc), lambda b_, c, I, i: (b_, I * nt + i, xcol + c)),
                  pl.BlockSpec((1, tc), lambda b_, c, I, i: (0, c))],
        out_specs=pl.BlockSpec((1, tt, tc), lambda b_, c, I, i: (b_, I * nt + i, c)),
        out_shape=jax.ShapeDtypeStruct((B, nb * P, C), out_dtype),
        compiler_params=pltpu.CompilerParams(dimension_semantics=("arbitrary",) * 4, vmem_limit_bytes=VMEM_LIMIT),
        name="hy_inv",
    )(yp, yq, a, b2, zarr, xarr, db)


def _hy_spectra(filt, a, b, P, mm):
    L, n_ord, _, C = filt.shape
    nb = L // P
    N = 2 * P
    assert P % 2 == 0
    hcat = jnp.concatenate([filt[:, o, s].reshape(nb, P, C) for o in range(n_ord) for s in range(2)], axis=0)
    ns = hcat.shape[0]
    flat = jnp.transpose(hcat, (1, 0, 2)).reshape(P, ns * C).astype(jnp.bfloat16)
    cs = mm(a, flat).reshape(P, ns, C)
    sn = mm(b, flat).reshape(P, ns, C)
    row0 = (jnp.arange(P) == 0)[:, None]
    sign = (1 - 2 * (jnp.arange(P) % 2)).astype(jnp.float32)[:, None]
    wgt = jnp.where(row0, 1.0 / N, 2.0 / N)
    zero = jnp.zeros((P, C), jnp.float32)
    out = []
    for o in range(n_ord):
        def fwd(s, j, drop_first):
            if j >= nb:
                return zero, zero
            k = (o * 2 + s) * nb + j
            x0 = hcat[k, 0:1, :].astype(jnp.bfloat16).astype(jnp.float32) if drop_first else 0.0
            return cs[:, k] - x0, sn[:, k] - jnp.where(row0, x0, 0.0)

        def bwd(s, j):
            c, q = fwd(s, j, True)
            return sign * c, jnp.where(row0, q, -sign * q)

        def first(s, j):
            if j >= nb:
                return zero, zero
            k = (o * 2 + s) * nb + j
            x0 = jnp.broadcast_to(hcat[k, 0:1, :].astype(jnp.bfloat16).astype(jnp.float32), (P, C))
            return x0, jnp.where(row0, x0, 0.0)

        kps, kqs = [], []
        for e in range(-(nb - 1), nb):
            if e >= 1:
                cp, qp = fwd(0, e, False)
                cn, qn = bwd(0, e - 1)
            elif e == 0:
                cp, qp = fwd(0, 0, False)
                cn, qn = fwd(1, 0, True)
            else:
                g = -e
                c1, q1 = bwd(1, g - 1)
                c2, q2 = first(1, g)
                cp, qp = c1 + c2, q1 + q2
                cn, qn = fwd(1, g, True)
            kps.append((cp + cn) * wgt)
            kqs.append(jnp.where(row0, qp + qn, qp - qn) * wgt)
        out.append((jnp.stack(kps), jnp.stack(kqs)))
    return out


def hyena(sc, filt, dbias, mm, P=HY_BLOCK, tc=512):
    L = sc.shape[1]
    C = filt.shape[3]
    P = min(P, L)
    a, b, b2 = _dft_mats(P)
    (kp1, kq1), (kp2, kq2) = _hy_spectra(filt, a, b, P, mm)
    db = dbias.astype(jnp.float32)
    nbc = C // tc
    yp, yq = _hy_fwd(sc, 0, a, b, kp1, kq1, tc=tc)
    z2 = _hy_inv(yp, yq, a, b2, sc, 0, sc, nbc, db[0:1], jnp.float32, tc=tc)
    yp, yq = _hy_fwd(z2, 0, a, b, kp2, kq2, tc=tc)
    return _hy_inv(yp, yq, a, b2, z2, 0, sc, 2 * nbc, db[1:2], jnp.bfloat16, tc=tc)


def _split_dot(x, ones_bf16):
    hi = x.astype(jnp.bfloat16)
    lo = (x - hi.astype(jnp.float32)).astype(jnp.bfloat16)
    return (jnp.dot(hi, ones_bf16, preferred_element_type=jnp.float32)
            + jnp.dot(lo, ones_bf16, preferred_element_type=jnp.float32))


def _head_ones():
    r = lax.broadcasted_iota(jnp.int32, (PAIR, PAIR), 0) // HEAD
    c = lax.broadcasted_iota(jnp.int32, (PAIR, PAIR), 1) // HEAD
    return (r == c).astype(jnp.bfloat16)


def _group_sum(x, ones):
    return jnp.concatenate([_split_dot(x[:, p * PAIR:(p + 1) * PAIR], ones) for p in range(x.shape[1] // PAIR)], axis=1)


def _softplus(x):
    return jnp.maximum(x, 0.0) + jnp.log1p(jnp.exp(-jnp.abs(x)))


def _wkv_fused_kernel(r_ref, k_ref, v_ref, lo_ref, w0_ref, a0_ref, w2_ref, a2_ref, kk_ref, ka_ref, o_ref, h_ref):
    rev = pl.program_id(0)

    @pl.when(pl.program_id(2) == 0)
    def _():
        h_ref[...] = jnp.zeros_like(h_ref)

    f32 = jnp.float32
    r = r_ref[0]
    k = k_ref[0]
    v = v_ref[0]
    xw = lo_ref[0, :, 0:PAIR]
    xa = lo_ref[0, :, PAIR:2 * PAIR]
    dw = jnp.dot(jnp.tanh(xw).astype(jnp.bfloat16), w2_ref[0], preferred_element_type=f32)
    lw = -jnp.exp(-_softplus(-(w0_ref[0] + dw)) - 0.5)
    a = jax.nn.sigmoid(a0_ref[0] + jnp.dot(xa.astype(jnp.bfloat16), a2_ref[0], preferred_element_type=f32))
    kq = k * kk_ref[...]
    ones = _head_ones()
    sq = (kq * kq).astype(jnp.bfloat16)
    ss = jnp.concatenate([jnp.dot(sq[:, p * PAIR:(p + 1) * PAIR], ones, preferred_element_type=f32)
                          for p in range(D_RW // PAIR)], axis=1)
    kk = kq / jnp.maximum(jnp.sqrt(ss), 1e-12)
    kdir = k * (1.0 + (a - 1.0) * ka_ref[...])
    kka = kk * a
    n = D_RW // PAIR
    sls = [slice(p * PAIR, (p + 1) * PAIR) for p in range(n)]
    ys, hn = _wkv_chunk([r[:, s] for s in sls], [lw[:, s] for s in sls], [kdir[:, s] for s in sls],
                        [v[:, s] for s in sls], [kk[:, s] for s in sls], [kka[:, s] for s in sls],
                        [h_ref[p] for p in range(n)], rev)
    for p in range(n):
        o_ref[0, 0, :, sls[p]] = ys[p]
        h_ref[p] = hn[p]


def wkv7_fused(sc, rcol, lcol, w0, a0, w2p, a2p, k_k, k_a):
    B, L, _ = sc.shape
    C = WKV_CHUNK
    nc = L // C
    D = D_RW
    assert L % C == 0
    cm = lambda d, c: c + d * (nc - 1 - 2 * c)
    col = lambda j: pl.BlockSpec((1, C, D), lambda d, b, c: (b, cm(d, c), j))
    vec2 = pl.BlockSpec((1, 1, D), lambda d, b, c: (d, 0, 0))
    mat2 = pl.BlockSpec((1, PAIR, D), lambda d, b, c: (d, 0, 0))
    vec = pl.BlockSpec((1, D), lambda d, b, c: (0, 0))
    return pl.pallas_call(
        _wkv_fused_kernel,
        grid=(2, B, nc),
        in_specs=[col(rcol), col(rcol + 1), col(rcol + 2),
                  pl.BlockSpec((1, C, LORA_W), lambda d, b, c: (b, cm(d, c), lcol)),
                  vec2, vec2, mat2, mat2, vec, vec],
        out_specs=pl.BlockSpec((1, 1, C, D), lambda d, b, c: (d, b, cm(d, c), 0)),
        out_shape=jax.ShapeDtypeStruct((2, B, L, D), jnp.float32),
        scratch_shapes=[pltpu.VMEM((D // PAIR, PAIR, PAIR), jnp.float32)],
        compiler_params=pltpu.CompilerParams(dimension_semantics=("arbitrary",) * 3, vmem_limit_bytes=VMEM_LIMIT),
        name="wkv7",
    )(sc, sc, sc, sc, w0, a0, w2p, a2p, k_k, k_a)


def _rw_post_kernel(y_ref, r_ref, k_ref, v_ref, lo_ref, a0_ref, a2_ref, ka_ref, rk_ref, lnw_ref, lnb_ref, g2_ref, wo_ref, o_ref):
    f32 = jnp.float32
    ones = _head_ones()
    y = y_ref[0, 0] + y_ref[1, 0]
    mu = _group_sum(y, ones) * (1.0 / HEAD)
    yc = y - mu
    var = _group_sum(yc * yc, ones) * (1.0 / HEAD)
    out = yc * lax.rsqrt(var + RW_GN_EPS) * lnw_ref[...] + lnb_ref[...]
    r = r_ref[0]
    k = k_ref[0]
    v = v_ref[0]
    xa = lo_ref[0, :, PAIR:2 * PAIR].astype(jnp.bfloat16)
    for d in range(2):
        a = jax.nn.sigmoid(a0_ref[d] + jnp.dot(xa, a2_ref[d], preferred_element_type=f32))
        kdir = k * (1.0 + (a - 1.0) * ka_ref[...])
        out = out + _group_sum(r * kdir * rk_ref[d], ones) * v
    g = jnp.dot(jax.nn.sigmoid(lo_ref[0]).astype(jnp.bfloat16), g2_ref[...], preferred_element_type=f32)
    o_ref[0] = jnp.dot((out * g).astype(jnp.bfloat16), wo_ref[...], preferred_element_type=f32).astype(o_ref.dtype)


def rw_post(y2, sc, rcol, lcol, a0, a2p, k_a, r_k, ln_w, ln_b, g2p, w_o, tm=256):
    _, B, L, D = y2.shape
    DM = w_o.shape[1]
    tm = min(tm, L)
    col = lambda j: pl.BlockSpec((1, tm, D), lambda b, i: (b, i, j))
    full = lambda shape: pl.BlockSpec(shape, lambda b, i: (0,) * len(shape))
    return pl.pallas_call(
        _rw_post_kernel,
        grid=(B, L // tm),
        in_specs=[pl.BlockSpec((2, 1, tm, D), lambda b, i: (0, b, i, 0)),
                  col(rcol), col(rcol + 1), col(rcol + 2),
                  pl.BlockSpec((1, tm, LORA_W), lambda b, i: (b, i, lcol)),
                  full((2, 1, D)), full((2, PAIR, D)), full((1, D)), full((2, 1, D)), full((1, D)), full((1, D)),
                  full((LORA_W, D)), full((D, DM))],
        out_specs=pl.BlockSpec((1, tm, DM), lambda b, i: (b, i, 0)),
        out_shape=jax.ShapeDtypeStruct((B, L, DM), jnp.bfloat16),
        compiler_params=pltpu.CompilerParams(dimension_semantics=("arbitrary",) * 2, vmem_limit_bytes=VMEM_LIMIT),
        name="rw_post",
    )(y2, sc, sc, sc, sc, a0, a2p, k_a, r_k, ln_w, ln_b, g2p, w_o)


def rwkv_params(p):
    f32, bf = jnp.float32, jnp.bfloat16
    z64 = jnp.zeros((64, D_RW), f32)
    w2p = jnp.stack([jnp.concatenate([p['rw_w2'][0], z64]), jnp.concatenate([z64, p['rw_w2'][1]])]).astype(bf)
    a2p = jnp.stack([jnp.concatenate([p['rw_a2'][0], z64]), jnp.concatenate([z64, p['rw_a2'][1]])]).astype(bf)
    g2p = jnp.zeros((LORA_W, D_RW), f32).at[256:256 + 160].set(p['rw_g2']).astype(bf)
    return dict(w0=p['rw_w0'].reshape(2, 1, D_RW), a0=p['rw_a0'].reshape(2, 1, D_RW), w2p=w2p, a2p=a2p, g2p=g2p,
                k_k=p['rw_k_k'].reshape(1, D_RW), k_a=p['rw_k_a'].reshape(1, D_RW), r_k=p['rw_r_k'].reshape(2, 1, D_RW),
                ln_w=p['rw_ln_w'].reshape(1, D_RW), ln_b=p['rw_ln_b'].reshape(1, D_RW), w_o=p['rw_w_o'].astype(bf))


def rwkv(sc, rcol, lcol, q):
    y2 = wkv7_fused(sc, rcol, lcol, q['w0'], q['a0'], q['w2p'], q['a2p'], q['k_k'], q['k_a'])
    return rw_post(y2, sc, rcol, lcol, q['a0'], q['a2p'], q['k_a'], q['r_k'], q['ln_w'], q['ln_b'], q['g2p'], q['w_o'])


def _attn_kernel(q_ref, k_ref, v_ref, wo_ref, o_ref):
    hp = pl.program_id(2)

    @pl.when(hp == 0)
    def _():
        o_ref[...] = jnp.zeros_like(o_ref)

    outs = []
    for h in range(CB // CA_HEAD):
        sl = slice(h * CA_HEAD, (h + 1) * CA_HEAD)
        q = q_ref[0, :, sl].astype(jnp.bfloat16)
        s = lax.dot_general(q, k_ref[0, :, sl].astype(jnp.bfloat16), (((1,), (1,)), ((), ())),
                            preferred_element_type=jnp.float32) * (CA_HEAD ** -0.5)
        s = s - jnp.max(s, axis=-1, keepdims=True)
        e = jnp.exp(s)
        p = e / jnp.sum(e, axis=-1, keepdims=True)
        outs.append(jnp.dot(p.astype(jnp.bfloat16), v_ref[0, :, sl].astype(jnp.bfloat16), preferred_element_type=jnp.float32))
    att = jnp.concatenate(outs, axis=1).astype(jnp.bfloat16)
    o_ref[0] += jnp.dot(att, wo_ref[...], preferred_element_type=jnp.float32)


def mem_attention(proj, qcol, kv, w_o, tm=512):
    B, L, _ = proj.shape
    M = kv.shape[1]
    DM = w_o.shape[1]
    nhp = w_o.shape[0] // CB
    tm = min(tm, L)
    return pl.pallas_call(
        _attn_kernel,
        grid=(B, L // tm, nhp),
        in_specs=[pl.BlockSpec((1, tm, CB), lambda b, i, h: (b, i, qcol + h)),
                  pl.BlockSpec((1, M, CB), lambda b, i, h: (b, 0, h)),
                  pl.BlockSpec((1, M, CB), lambda b, i, h: (b, 0, nhp + h)),
                  pl.BlockSpec((CB, DM), lambda b, i, h: (h, 0))],
        out_specs=pl.BlockSpec((1, tm, DM), lambda b, i, h: (b, i, 0)),
        out_shape=jax.ShapeDtypeStruct((B, L, DM), jnp.float32),
        compiler_params=pltpu.CompilerParams(dimension_semantics=("arbitrary",) * 3, vmem_limit_bytes=VMEM_LIMIT),
        name="mem_attn",
    )(proj, kv, kv, w_o)


def _merge_kernel(hy_ref, rw_ref, ca_ref, g0_ref, g1_ref, g2_ref, wo_ref, x_ref, lg_ref, lb_ref, wr_ref,
                  o_ref, ob_ref, lo_ref):
    f32 = jnp.float32
    m = (jax.nn.sigmoid(g0_ref[0].astype(f32)) * hy_ref[0] + jax.nn.sigmoid(g1_ref[0].astype(f32)) * rw_ref[0]
         + jax.nn.sigmoid(g2_ref[0].astype(f32)) * ca_ref[0])
    h = DN_ALPHA * x_ref[0] + jnp.dot(m.astype(jnp.bfloat16), wo_ref[...], preferred_element_type=f32)
    mu = jnp.mean(h, axis=-1, keepdims=True)
    hc = h - mu
    var = jnp.mean(hc * hc, axis=-1, keepdims=True)
    y = hc * lax.rsqrt(var + LN_EPS) * lg_ref[...] + lb_ref[...]
    o_ref[0] = DN_ALPHA * y
    yb = y.astype(jnp.bfloat16)
    ob_ref[0] = yb
    lo_ref[0] = jnp.dot(yb, wr_ref[...], preferred_element_type=f32)


def merge_ln(hy, rw, ca, pq, gcol, w_out, x, ln_g, ln_b, w_router_pad, tm=256):
    B, L, DM = x.shape
    tm = min(tm, L)
    row = pl.BlockSpec((1, tm, DM), lambda b, i: (b, i, 0))
    gate = lambda g: pl.BlockSpec((1, tm, DM), lambda b, i: (b, i, gcol + g))
    vec = pl.BlockSpec((1, DM), lambda b, i: (0, 0))
    return pl.pallas_call(
        _merge_kernel,
        grid=(B, L // tm),
        in_specs=[row, row, row, gate(0), gate(1), gate(2),
                  pl.BlockSpec((DM, DM), lambda b, i: (0, 0)), row, vec, vec,
                  pl.BlockSpec((DM, LANES), lambda b, i: (0, 0))],
        out_specs=[row, row, pl.BlockSpec((1, tm, LANES), lambda b, i: (b, i, 0))],
        out_shape=[jax.ShapeDtypeStruct((B, L, DM), jnp.float32), jax.ShapeDtypeStruct((B, L, DM), jnp.bfloat16),
                   jax.ShapeDtypeStruct((B, L, LANES), jnp.float32)],
        compiler_params=pltpu.CompilerParams(dimension_semantics=("arbitrary",) * 2, vmem_limit_bytes=VMEM_LIMIT),
        name="merge_ln",
    )(hy, rw, ca, pq, pq, pq, w_out, x, ln_g, ln_b, w_router_pad)


COMBINE_ROWS = 1024
COMBINE_UNROLL = 16


def _row_copy(y_ref, buf, row, r, sem, to_hbm):
    hbm = y_ref.at[pl.ds(row, 1)]
    vm = buf.at[pl.ds(r, 1)]
    return pltpu.make_async_copy(vm, hbm, sem) if to_hbm else pltpu.make_async_copy(hbm, vm, sem)


def _combine_kernel(idx_ref, out_ref, yin_ref, y_ref, buf, sem_in, sem_out, *, R):
    del yin_ref

    def start_in(r, c):
        _row_copy(y_ref, buf, idx_ref[0, 0, r], r, sem_in, False).start()
        return c

    def wait_in(r, c):
        _row_copy(y_ref, buf, idx_ref[0, 0, r], r, sem_in, False).wait()
        return c

    def start_out(r, c):
        _row_copy(y_ref, buf, idx_ref[0, 0, r], r, sem_out, True).start()
        return c

    def wait_out(r, c):
        _row_copy(y_ref, buf, idx_ref[0, 0, r], r, sem_out, True).wait()
        return c

    lax.fori_loop(0, R, start_in, 0, unroll=COMBINE_UNROLL)
    lax.fori_loop(0, R, wait_in, 0, unroll=COMBINE_UNROLL)
    buf[...] = buf[...] + out_ref[0]
    lax.fori_loop(0, R, start_out, 0, unroll=COMBINE_UNROLL)
    lax.fori_loop(0, R, wait_out, 0, unroll=COMBINE_UNROLL)


def moe_combine(idx, out, y0, R=COMBINE_ROWS):
    E, C, D = out.shape
    R = min(R, C)
    assert C % R == 0 and R % COMBINE_UNROLL == 0
    nch = C // R
    return pl.pallas_call(
        functools.partial(_combine_kernel, R=R),
        grid=(E, nch),
        in_specs=[pl.BlockSpec((1, 1, R), lambda e, c: (e * nch + c, 0, 0), memory_space=pltpu.SMEM),
                  pl.BlockSpec((1, R, D), lambda e, c: (e, c, 0)),
                  pl.BlockSpec(memory_space=pl.ANY)],
        out_specs=pl.BlockSpec(memory_space=pl.ANY),
        out_shape=jax.ShapeDtypeStruct(y0.shape, y0.dtype),
        scratch_shapes=[pltpu.VMEM((R, D), jnp.float32), pltpu.SemaphoreType.DMA, pltpu.SemaphoreType.DMA],
        input_output_aliases={2: 0},
        compiler_params=pltpu.CompilerParams(dimension_semantics=("arbitrary", "arbitrary"), vmem_limit_bytes=VMEM_LIMIT,
                                             has_side_effects=True),
        name="moe_combine",
    )(idx.reshape(E * nch, 1, R), out, y0)


def _ln_kernel(x_ref, g_ref, b_ref, o_ref):
    h = x_ref[...]
    mu = jnp.mean(h, axis=-1, keepdims=True)
    hc = h - mu
    var = jnp.mean(hc * hc, axis=-1, keepdims=True)
    o_ref[...] = hc * lax.rsqrt(var + LN_EPS) * g_ref[...] + b_ref[...]


def layer_norm(x, g, b, tm=512):
    T, DM = x.shape
    row = pl.BlockSpec((tm, DM), lambda i: (i, 0))
    vec = pl.BlockSpec((1, DM), lambda i: (0, 0))
    return pl.pallas_call(
        _ln_kernel, grid=(T // tm,), in_specs=[row, vec, vec], out_specs=row,
        out_shape=jax.ShapeDtypeStruct((T, DM), jnp.float32),
        compiler_params=pltpu.CompilerParams(dimension_semantics=("arbitrary",), vmem_limit_bytes=VMEM_LIMIT),
        name="final_ln",
    )(x, g, b)


def _hyena_filters(L, w1, b1, w2, b2, w3, b3, w4, sin_freq):
    f32 = jnp.float32
    t = jnp.linspace(0.0, 1.0, L, dtype=f32)[:, None]
    ang = 2.0 * math.pi * jnp.arange(L, dtype=f32)[:, None] / L
    bands = jnp.linspace(1e-4, HY_BANDS - 1, HY_BANDS, dtype=f32)[None]
    z = jnp.concatenate([t, jnp.cos(bands * ang), -jnp.sin(bands * ang)], axis=-1)
    sf = sin_freq.astype(f32)
    h = jnp.sin(sf[0] * (z @ w1.astype(f32) + b1.astype(f32)))
    h = jnp.sin(sf[1] * (h @ w2.astype(f32) + b2.astype(f32)))
    h = jnp.sin(sf[2] * (h @ w3.astype(f32) + b3.astype(f32)))
    h = (h @ w4.astype(f32)).reshape(L, HY_ORDER, 2, D_HY)
    deltas = jnp.abs(jnp.linspace(math.log(HY_DECAY_TARGET) / HY_SLOW_DECAY_PCT,
                                  math.log(HY_DECAY_TARGET) / HY_FAST_DECAY_PCT, D_HY, dtype=f32))
    window = jnp.exp(-t * deltas[None])
    return h * window[:, None, None, :]


def _expert_choice_moe(resid, xb, logits, w_gate, w_up, w_down):
    T, D = xb.shape
    cap = (EC_CAPACITY * T) // N_EXPERTS
    aff = jax.nn.softmax(logits.astype(jnp.float32), axis=-1)
    gval, idx = lax.top_k(aff.T, cap)
    out = _moe_ffn(xb[idx], jnp.broadcast_to(gval[..., None], gval.shape + (LANES,)), w_gate, w_up, w_down)
    return moe_combine(idx, out, resid)


def _layer(x, mem, p):
    B, L, D = x.shape
    xb = x.astype(jnp.bfloat16)
    sc = proj_conv(xb, p['w_conv'], p['w_short_all'])
    pq = _mm3(xb, p['w_rest'], out_dtype=jnp.bfloat16)
    filt = _hyena_filters(L, p['hy_w1'], p['hy_b1'], p['hy_w2'], p['hy_b2'], p['hy_w3'], p['hy_b3'],
                          p['hy_w4'], p['hy_sin_freq'])
    hy = _mm3(hyena(sc, filt, p['hy_dbias'], _matmul), p['hy_w_o'], out_dtype=jnp.bfloat16)
    rw = rwkv(sc, P_RW // D_RW, P_LORA // LORA_W, p['rw'])
    ca = mem_attention(pq, N_BRANCH * D_MODEL // CB, _mm3(mem, p['ca_w_kv']), p['ca_w_o'])
    xs, x1b, logits = merge_ln(hy, rw, ca, pq, 0, p['w_out'], x, p['ln1_g'], p['ln1_b'], p['w_router_pad'])
    h = _expert_choice_moe(xs.reshape(B * L, D), x1b.reshape(B * L, D), logits.reshape(B * L, LANES)[:, :N_EXPERTS],
                           p['moe_w_gate'], p['moe_w_up'], p['moe_w_down'])
    return layer_norm(h, p['ln2_g'], p['ln2_b']).reshape(B, L, D)


def kernel(x_prompt, x_sample, mem_prompt, mem_sample, w_in, w_short, hy_w1, hy_b1, hy_w2, hy_b2, hy_w3, hy_b3, hy_w4, hy_sin_freq, hy_dbias, hy_w_o, rw_w0, rw_w2, rw_a0, rw_a2, rw_g2, rw_k_k, rw_k_a, rw_r_k, rw_ln_w, rw_ln_b, rw_w_o, ca_w_kv, ca_w_o, w_out, ln1_g, ln1_b, moe_w_router, moe_w_gate, moe_w_up, moe_w_down, ln2_g, ln2_b):
    params = dict(w_in=w_in, w_short=w_short, hy_w1=hy_w1, hy_b1=hy_b1, hy_w2=hy_w2, hy_b2=hy_b2,
                  hy_w3=hy_w3, hy_b3=hy_b3, hy_w4=hy_w4, hy_sin_freq=hy_sin_freq, hy_dbias=hy_dbias,
                  hy_w_o=hy_w_o, rw_w0=rw_w0, rw_w2=rw_w2, rw_a0=rw_a0, rw_a2=rw_a2, rw_g2=rw_g2,
                  rw_k_k=rw_k_k, rw_k_a=rw_k_a, rw_r_k=rw_r_k, rw_ln_w=rw_ln_w, rw_ln_b=rw_ln_b,
                  rw_w_o=rw_w_o, ca_w_kv=ca_w_kv, ca_w_o=ca_w_o, w_out=w_out, ln1_g=ln1_g, ln1_b=ln1_b,
                  moe_w_router=moe_w_router, moe_w_gate=moe_w_gate, moe_w_up=moe_w_up,
                  moe_w_down=moe_w_down, ln2_g=ln2_g, ln2_b=ln2_b)
    bf = jnp.bfloat16
    y_prompt, y_sample = x_prompt, x_sample
    for i in range(DEPTH):
        lp = {name: arr[i] for name, arr in params.items()}
        wi = lp['w_in']
        lp['w_conv'] = jnp.pad(wi[:, :OFF_Q], ((0, 0), (0, N_LORA_PAD - N_LORA))).astype(bf)
        lp['w_rest'] = jnp.concatenate([wi[:, OFF_GATE:], wi[:, OFF_Q:OFF_GATE]], axis=1).astype(bf)
        lp['w_short_all'] = jnp.pad(lp['w_short'], ((0, 0), (0, N_LORA_PAD - N_LORA)))
        for name in ('hy_w_o', 'ca_w_o', 'ca_w_kv', 'w_out'):
            lp[name] = lp[name].astype(bf)
        lp['rw'] = rwkv_params(lp)
        lp['w_router_pad'] = jnp.pad(lp['moe_w_router'], ((0, 0), (0, LANES - N_EXPERTS))).astype(bf)
        for name in ('ln1_g', 'ln1_b', 'ln2_g', 'ln2_b'):
            lp[name] = lp[name].reshape(1, D_MODEL)
        y_prompt = _layer(y_prompt, mem_prompt, lp)
        y_sample = _layer(y_sample, mem_sample, lp)
    return (y_prompt, y_sample)
```

```python
import functools
import math

import jax
import jax.numpy as jnp
from jax import lax
from jax.experimental import pallas as pl
from jax.experimental.pallas import tpu as pltpu

D_MODEL = 2048
DEPTH = 1
D_HY = D_MODEL // 2
HY_ORDER = 2
HY_EMB = 33
HY_BANDS = (HY_EMB - 1) // 2
HY_FFN = 64
HY_FAST_DECAY_PCT = 0.3
HY_SLOW_DECAY_PCT = 1.5
HY_DECAY_TARGET = 1e-2
D_RW = D_MODEL // 2
RW_HEAD = 64
RW_HEADS = D_RW // RW_HEAD
RW_DECAY_LORA = 64
RW_AAA_LORA = 64
RW_GATE_LORA = 160
RW_GN_EPS = 64e-5
D_CA = D_MODEL // 2
CA_HEADS = 4
CA_HEAD = D_CA // CA_HEADS
N_BRANCH = 3
N_EXPERTS = 16
EC_CAPACITY = 2
D_EXPERT = 2 * D_MODEL
LN_EPS = 1e-5
DN_ALPHA = (2 * DEPTH) ** 0.25

OFF_HY = 0
OFF_RW = OFF_HY + 3 * D_HY
OFF_LORA = OFF_RW + 3 * D_RW
N_LORA = 2 * RW_DECAY_LORA + 2 * RW_AAA_LORA + RW_GATE_LORA
OFF_Q = OFF_LORA + N_LORA
OFF_GATE = OFF_Q + D_CA
N_IN = OFF_GATE + N_BRANCH * D_MODEL

LANES = 128
N_LORA_PAD = -(-N_LORA // LANES) * LANES
P_HY = 0
P_RW = P_HY + 3 * D_HY
P_LORA = P_RW + 3 * D_RW
P_Q = P_LORA + N_LORA_PAD
P_GATE = P_Q + D_CA
P_ALL = P_GATE + N_BRANCH * D_MODEL

VMEM_LIMIT = 56 * 1024 * 1024
SUBLANES = 8
CB = 512
N_CONV = P_Q
LORA_W = N_LORA_PAD


def _mm_kernel(a_ref, b_ref, o_ref):
    o_ref[...] = jnp.dot(a_ref[...].astype(jnp.bfloat16), b_ref[...].astype(jnp.bfloat16),
                         preferred_element_type=jnp.float32).astype(o_ref.dtype)


def _matmul(a, b, tm=1024, tn=512, out_dtype=jnp.float32):
    M, K = a.shape
    N = b.shape[1]
    tm = min(tm, M)
    tn = min(tn, N)
    assert M % tm == 0 and N % tn == 0, (M, N, tm, tn)
    return pl.pallas_call(
        _mm_kernel,
        grid=(M // tm, N // tn),
        in_specs=[pl.BlockSpec((tm, K), lambda i, j: (i, 0)),
                  pl.BlockSpec((K, tn), lambda i, j: (0, j))],
        out_specs=pl.BlockSpec((tm, tn), lambda i, j: (i, j)),
        out_shape=jax.ShapeDtypeStruct((M, N), out_dtype),
        compiler_params=pltpu.CompilerParams(dimension_semantics=("arbitrary", "arbitrary"),
                                             vmem_limit_bytes=VMEM_LIMIT),
        name="mm",
    )(a, b)


def _mm3(x, w, **kw):
    B, L, K = x.shape
    return _matmul(x.reshape(B * L, K), w, **kw).reshape(B, L, w.shape[1])


def _moe_kernel(x_ref, gv_ref, wg_ref, wu_ref, wd_ref, o_ref):
    @pl.when(pl.program_id(2) == 0)
    def _():
        o_ref[...] = jnp.zeros_like(o_ref)

    x = x_ref[0]
    hg = jnp.dot(x, wg_ref[0].astype(jnp.bfloat16), preferred_element_type=jnp.float32)
    hu = jnp.dot(x, wu_ref[0].astype(jnp.bfloat16), preferred_element_type=jnp.float32)
    h = (hg * jax.nn.sigmoid(hg) * hu).astype(jnp.bfloat16)
    o_ref[0] += jnp.dot(h, wd_ref[0].astype(jnp.bfloat16), preferred_element_type=jnp.float32)

    @pl.when(pl.program_id(2) == pl.num_programs(2) - 1)
    def _():
        o_ref[0] = o_ref[0] * jnp.tile(gv_ref[0], (1, o_ref.shape[2] // LANES))


def _moe_ffn(xe, gv, w_gate, w_up, w_down, tm=1024, tf=256):
    E, C, D = xe.shape
    F = w_gate.shape[2]
    tm = min(tm, C)
    return pl.pallas_call(
        _moe_kernel,
        grid=(E, C // tm, F // tf),
        in_specs=[pl.BlockSpec((1, tm, D), lambda e, i, f: (e, i, 0)),
                  pl.BlockSpec((1, tm, LANES), lambda e, i, f: (e, i, 0)),
                  pl.BlockSpec((1, D, tf), lambda e, i, f: (e, 0, f)),
                  pl.BlockSpec((1, D, tf), lambda e, i, f: (e, 0, f)),
                  pl.BlockSpec((1, tf, D), lambda e, i, f: (e, f, 0))],
        out_specs=pl.BlockSpec((1, tm, D), lambda e, i, f: (e, i, 0)),
        out_shape=jax.ShapeDtypeStruct((E, C, D), jnp.float32),
        compiler_params=pltpu.CompilerParams(dimension_semantics=("arbitrary",) * 3, vmem_limit_bytes=VMEM_LIMIT),
        name="moe_ffn",
    )(xe, gv, w_gate, w_up, w_down)


WKV_CHUNK = 64
HEAD = RW_HEAD
PAIR = 2 * HEAD


def _bdot(a, b):
    return jnp.dot(a.astype(jnp.bfloat16), b.astype(jnp.bfloat16), preferred_element_type=jnp.float32)


def _bdot_nt(a, b):
    return lax.dot_general(a.astype(jnp.bfloat16), b.astype(jnp.bfloat16), (((1,), (1,)), ((), ())),
                           preferred_element_type=jnp.float32)


def _bdot_tn(a, b):
    return lax.dot_general(a.astype(jnp.bfloat16), b.astype(jnp.bfloat16), (((0,), (0,)), ((), ())),
                           preferred_element_type=jnp.float32)


def _split3(x):
    f32, bf = jnp.float32, jnp.bfloat16
    hi = x.astype(bf)
    r1 = x - hi.astype(f32)
    mid = r1.astype(bf)
    lo = (r1 - mid.astype(f32)).astype(bf)
    return hi, mid, lo


def _wkv_chunk(rs, lws, ks, vs, kks, kkas, ss, rev):
    C = WKV_CHUNK
    f32, bf = jnp.float32, jnp.bfloat16
    n = len(rs)
    ri = lax.broadcasted_iota(jnp.int32, (C, C), 0)
    ci = lax.broadcasted_iota(jnp.int32, (C, C), 1)
    dd = jnp.where(rev == 0, ri - ci, ci - ri)
    tri = (dd >= 0).astype(bf)
    lane = lax.broadcasted_iota(jnp.int32, (C, PAIR), 1)
    first = lane < HEAD
    r2 = lax.broadcasted_iota(jnp.int32, (2 * C, 2 * C), 0)
    c2 = lax.broadcasted_iota(jnp.int32, (2 * C, 2 * C), 1)
    d2 = jnp.where(rev == 0, r2 - c2, c2 - r2)
    strict = d2 > 0
    incl = d2 >= 0
    eye = (r2 == c2).astype(f32)

    def stack(x):
        return jnp.concatenate([jnp.where(first, x, 0.0), jnp.where(first, 0.0, x)], axis=0)

    cums = []
    for lw in lws:
        hi, mid, lo = _split3(lw)
        cums.append(jnp.dot(tri, hi, preferred_element_type=f32) + jnp.dot(tri, mid, preferred_element_type=f32)
                    + jnp.dot(tri, lo, preferred_element_type=f32))
    galls = [jnp.exp(jnp.sum(lw, axis=0, keepdims=True)) for lw in lws]
    lhs, rhs, vS, a2k2 = [], [], [], []
    for i in range(n):
        cum, lw = cums[i], lws[i]
        ginv = jnp.exp(-cum)
        bt = stack(kks[i] * jnp.exp(cum - lw))
        rt = stack(rs[i] * jnp.exp(cum))
        at = stack(-(kkas[i] * ginv))
        kt = stack(ks[i] * ginv)
        lhs.append(jnp.concatenate([bt, rt], axis=0).astype(bf))
        rhs.append(jnp.concatenate([at, kt], axis=0).astype(bf))
        a2k2.append((jnp.concatenate([at, kt], axis=0) * galls[i]).astype(bf))
        vS.append(stack(vs[i]).astype(bf))
    gram = [_bdot_nt(lhs[i], rhs[i]) for i in range(n)]
    C2 = 2 * C
    mab = [jnp.where(strict, g[:C2, :C2], 0.0) for g in gram]
    mbk = [jnp.where(strict, g[:C2, C2:], 0.0).astype(bf) for g in gram]
    mrr = [jnp.concatenate([jnp.where(incl, g[C2:, :C2], 0.0), jnp.where(incl, g[C2:, C2:], 0.0)], axis=1).astype(bf)
           for g in gram]
    ts = [eye + m for m in mab]
    ps = [_bdot(m, m) for m in mab]
    for _ in range(4):
        xs = [_bdot(jnp.concatenate([t, p], axis=0), p) for t, p in zip(ts, ps)]
        ts = [t + x[:C2] for t, x in zip(ts, xs)]
        ps = [x[C2:] for x in xs]
    ts = [t + _bdot(t, p) for t, p in zip(ts, ps)]
    sB = [s.astype(bf) for s in ss]
    hs = [_bdot_nt(lhs[i], sB[i]) for i in range(n)]
    wS = [hs[i][:C2] + _bdot(mbk[i], vS[i]) for i in range(n)]
    uS = [_bdot(ts[i], wS[i]).astype(bf) for i in range(n)]
    uv = [jnp.concatenate([uS[i], vS[i]], axis=0) for i in range(n)]
    yS = [hs[i][C2:] + _bdot(mrr[i], uv[i]) for i in range(n)]
    ys = [y[:C] + y[C:] for y in yS]
    sn = [ss[i] * galls[i] + _bdot_tn(uv[i], a2k2[i]) for i in range(n)]
    return ys, sn


HALO = 2 * SUBLANES


def _projconv_kernel(x_ref, h_ref, w_ref, ws_ref, o_ref):
    w = w_ref[...]
    p = jnp.dot(x_ref[...], w, preferred_element_type=jnp.float32)
    ph = jnp.dot(h_ref[0], w, preferred_element_type=jnp.float32)
    tm = p.shape[0]
    rid = lax.broadcasted_iota(jnp.int32, (tm, 1), 0)
    prev = jnp.where(rid == 0, ph[SUBLANES - 1:SUBLANES, :], pltpu.roll(p, 1, 0))
    nxt = jnp.where(rid == tm - 1, ph[SUBLANES:SUBLANES + 1, :], pltpu.roll(p, tm - 1, 0))
    o_ref[...] = ws_ref[0:1, :] * prev + ws_ref[1:2, :] * p + ws_ref[2:3, :] * nxt


def _halo_rows(xb, tm):
    B, L, K = xb.shape
    nt = L // tm
    xr = xb.reshape(B, nt, tm, K)
    zero = jnp.zeros((B, 1, SUBLANES, K), xb.dtype)
    above = jnp.concatenate([zero, xr[:, :-1, tm - SUBLANES:]], axis=1)
    below = jnp.concatenate([xr[:, 1:, :SUBLANES], zero], axis=1)
    return jnp.concatenate([above, below], axis=2).reshape(B * nt, HALO, K)


def proj_conv(xb, w, ws, tm=1024, tn=512):
    B, L, K = xb.shape
    N = w.shape[1]
    tm = min(tm, L)
    assert L % tm == 0 and N % tn == 0
    halo = _halo_rows(xb, tm)
    T = B * L
    out = pl.pallas_call(
        _projconv_kernel,
        grid=(T // tm, N // tn),
        in_specs=[pl.BlockSpec((tm, K), lambda i, j: (i, 0)),
                  pl.BlockSpec((1, HALO, K), lambda i, j: (i, 0, 0)),
                  pl.BlockSpec((K, tn), lambda i, j: (0, j)),
                  pl.BlockSpec((3, tn), lambda i, j: (0, j))],
        out_specs=pl.BlockSpec((tm, tn), lambda i, j: (i, j)),
        out_shape=jax.ShapeDtypeStruct((T, N), jnp.float32),
        compiler_params=pltpu.CompilerParams(dimension_semantics=("arbitrary", "arbitrary"), vmem_limit_bytes=VMEM_LIMIT),
        name="proj_conv",
    )(xb.reshape(T, K), halo, w, ws)
    return out.reshape(B, L, N)


def _dft_mats(L):
    N = 2 * L
    f = lax.broadcasted_iota(jnp.int32, (L, L), 0)
    s = lax.broadcasted_iota(jnp.int32, (L, L), 1)
    ang = ((f * s) % N).astype(jnp.float32) * (2.0 * math.pi / N)
    a = jnp.cos(ang)
    sn = jnp.sin(ang)
    alt_s = (1 - 2 * (s % 2)).astype(jnp.float32)
    alt_f = (1 - 2 * (f % 2)).astype(jnp.float32)
    b = jnp.where(f == 0, alt_s, sn)
    b2 = jnp.where(s == 0, alt_f, sn)
    return a.astype(jnp.bfloat16), b.astype(jnp.bfloat16), b2.astype(jnp.bfloat16)


HY_BLOCK = 1024


def _hy_fwd_kernel(z_ref, a_ref, b_ref, kp_ref, kq_ref, yp_ref, yq_ref, zb_ref, *, tf, nb, P):
    j = pl.program_id(2)

    @pl.when(j == 0)
    def _():
        zb_ref[...] = z_ref[0].astype(jnp.bfloat16)

    a = a_ref[...]
    b = b_ref[...]
    ps = [jnp.dot(a, zb_ref[J * P:(J + 1) * P, :], preferred_element_type=jnp.float32) for J in range(nb)]
    qs = [jnp.dot(b, zb_ref[J * P:(J + 1) * P, :], preferred_element_type=jnp.float32) for J in range(nb)]
    row = lax.broadcasted_iota(jnp.int32, ps[0].shape, 0) + j * tf
    packed = row == 0
    for I in range(nb):
        yp = None
        yq = None
        for J in range(nb):
            kp = kp_ref[I - J + nb - 1]
            kq = kq_ref[I - J + nb - 1]
            kqq = kq * qs[J]
            tp = kp * ps[J] - jnp.where(packed, 0.0, kqq)
            tq = jnp.where(packed, kqq, kp * qs[J] + kq * ps[J])
            yp = tp if yp is None else yp + tp
            yq = tq if yq is None else yq + tq
        yp_ref[0, I] = yp.astype(yp_ref.dtype)
        yq_ref[0, I] = yq.astype(yq_ref.dtype)


def _hy_fwd(z, zcol, a, b, kp, kq, tf=256, tc=512):
    B, L, _ = z.shape
    nk, P, C = kp.shape
    nb = (nk + 1) // 2
    assert nb * P == L
    tf = min(tf, P)
    zspec = pl.BlockSpec((1, L, tc), lambda b_, c, j: (b_, 0, zcol + c))
    mspec = pl.BlockSpec((tf, P), lambda b_, c, j: (j, 0))
    kspec = pl.BlockSpec((nk, tf, tc), lambda b_, c, j: (0, j, c))
    ospec = pl.BlockSpec((1, nb, tf, tc), lambda b_, c, j: (b_, 0, j, c))
    return pl.pallas_call(
        functools.partial(_hy_fwd_kernel, tf=tf, nb=nb, P=P),
        grid=(B, C // tc, P // tf),
        in_specs=[zspec, mspec, mspec, kspec, kspec],
        out_specs=[ospec, ospec],
        out_shape=[jax.ShapeDtypeStruct((B, nb, P, C), jnp.bfloat16)] * 2,
        scratch_shapes=[pltpu.VMEM((L, tc), jnp.bfloat16)],
        compiler_params=pltpu.CompilerParams(dimension_semantics=("arbitrary",) * 3, vmem_limit_bytes=VMEM_LIMIT),
        name="hy_fwd",
    )(z, a, b, kp, kq)


def _hy_inv_kernel(yp_ref, yq_ref, a_ref, b2_ref, z_ref, x_ref, db_ref, o_ref):
    y = jnp.dot(a_ref[...], yp_ref[0, 0], preferred_element_type=jnp.float32)
    y += jnp.dot(b2_ref[...], yq_ref[0, 0], preferred_element_type=jnp.float32)
    o_ref[0] = (x_ref[0] * (y + z_ref[0] * db_ref[...])).astype(o_ref.dtype)


def _hy_inv(yp, yq, a, b2, zarr, zcol, xarr, xcol, db, out_dtype, tt=512, tc=512):
    B, nb, P, C = yp.shape
    tt = min(tt, P)
    nt = P // tt
    yspec = pl.BlockSpec((1, 1, P, tc), lambda b_, c, I, i: (b_, I, 0, c))
    mspec = pl.BlockSpec((tt, P), lambda b_, c, I, i: (i, 0))
    return pl.pallas_call(
        _hy_inv_kernel,
        grid=(B, C // tc, nb, nt),
        in_specs=[yspec, yspec, mspec, mspec,
                  pl.BlockSpec((1, tt, tc), lambda b_, c, I, i: (b_, I * nt + i, zcol + c)),
                  pl.BlockSpec((1, tt, tc), lambda b_, c, I, i: (b_, I * nt + i, xcol + c)),
                  pl.BlockSpec((1, tc), lambda b_, c, I, i: (0, c))],
        out_specs=pl.BlockSpec((1, tt, tc), lambda b_, c, I, i: (b_, I * nt + i, c)),
        out_shape=jax.ShapeDtypeStruct((B, nb * P, C), out_dtype),
        compiler_params=pltpu.CompilerParams(dimension_semantics=("arbitrary",) * 4, vmem_limit_bytes=VMEM_LIMIT),
        name="hy_inv",
    )(yp, yq, a, b2, zarr, xarr, db)


def _hy_spectra(filt, a, b, P, mm):
    L, n_ord, _, C = filt.shape
    nb = L // P
    N = 2 * P
    assert P % 2 == 0
    hcat = jnp.concatenate([filt[:, o, s].reshape(nb, P, C) for o in range(n_ord) for s in range(2)], axis=0)
    ns = hcat.shape[0]
    flat = jnp.transpose(hcat, (1, 0, 2)).reshape(P, ns * C).astype(jnp.bfloat16)
    cs = mm(a, flat).reshape(P, ns, C)
    sn = mm(b, flat).reshape(P, ns, C)
    row0 = (jnp.arange(P) == 0)[:, None]
    sign = (1 - 2 * (jnp.arange(P) % 2)).astype(jnp.float32)[:, None]
    wgt = jnp.where(row0, 1.0 / N, 2.0 / N)
    zero = jnp.zeros((P, C), jnp.float32)
    out = []
    for o in range(n_ord):
        def fwd(s, j, drop_first):
            if j >= nb:
                return zero, zero
            k = (o * 2 + s) * nb + j
            x0 = hcat[k, 0:1, :].astype(jnp.bfloat16).astype(jnp.float32) if drop_first else 0.0
            return cs[:, k] - x0, sn[:, k] - jnp.where(row0, x0, 0.0)

        def bwd(s, j):
            c, q = fwd(s, j, True)
            return sign * c, jnp.where(row0, q, -sign * q)

        def first(s, j):
            if j >= nb:
                return zero, zero
            k = (o * 2 + s) * nb + j
            x0 = jnp.broadcast_to(hcat[k, 0:1, :].astype(jnp.bfloat16).astype(jnp.float32), (P, C))
            return x0, jnp.where(row0, x0, 0.0)

        kps, kqs = [], []
        for e in range(-(nb - 1), nb):
            if e >= 1:
                cp, qp = fwd(0, e, False)
                cn, qn = bwd(0, e - 1)
            elif e == 0:
                cp, qp = fwd(0, 0, False)
                cn, qn = fwd(1, 0, True)
            else:
                g = -e
                c1, q1 = bwd(1, g - 1)
                c2, q2 = first(1, g)
                cp, qp = c1 + c2, q1 + q2
                cn, qn = fwd(1, g, True)
            kps.append((cp + cn) * wgt)
            kqs.append(jnp.where(row0, qp + qn, qp - qn) * wgt)
        out.append((jnp.stack(kps), jnp.stack(kqs)))
    return out


def hyena(sc, filt, dbias, mm, P=HY_BLOCK, tc=512):
    L = sc.shape[1]
    C = filt.shape[3]
    P = min(P, L)
    a, b, b2 = _dft_mats(P)
    (kp1, kq1), (kp2, kq2) = _hy_spectra(filt, a, b, P, mm)
    db = dbias.astype(jnp.float32)
    nbc = C // tc
    yp, yq = _hy_fwd(sc, 0, a, b, kp1, kq1, tc=tc)
    z2 = _hy_inv(yp, yq, a, b2, sc, 0, sc, nbc, db[0:1], jnp.float32, tc=tc)
    yp, yq = _hy_fwd(z2, 0, a, b, kp2, kq2, tc=tc)
    return _hy_inv(yp, yq, a, b2, z2, 0, sc, 2 * nbc, db[1:2], jnp.bfloat16, tc=tc)


def _split_dot(x, ones_bf16):
    hi = x.astype(jnp.bfloat16)
    lo = (x - hi.astype(jnp.float32)).astype(jnp.bfloat16)
    return (jnp.dot(hi, ones_bf16, preferred_element_type=jnp.float32)
            + jnp.dot(lo, ones_bf16, preferred_element_type=jnp.float32))


def _head_ones():
    r = lax.broadcasted_iota(jnp.int32, (PAIR, PAIR), 0) // HEAD
    c = lax.broadcasted_iota(jnp.int32, (PAIR, PAIR), 1) // HEAD
    return (r == c).astype(jnp.bfloat16)


def _group_sum(x, ones):
    return jnp.concatenate([_split_dot(x[:, p * PAIR:(p + 1) * PAIR], ones) for p in range(x.shape[1] // PAIR)], axis=1)


def _softplus(x):
    return jnp.maximum(x, 0.0) + jnp.log1p(jnp.exp(-jnp.abs(x)))


def _wkv_fused_kernel(r_ref, k_ref, v_ref, lo_ref, w0_ref, a0_ref, w2_ref, a2_ref, kk_ref, ka_ref, o_ref, h_ref, *, nbat):
    rev = pl.program_id(0)

    @pl.when(pl.program_id(2) == 0)
    def _():
        h_ref[...] = jnp.zeros_like(h_ref)

    f32 = jnp.float32
    C = WKV_CHUNK
    rows = nbat * C
    r = r_ref[...].reshape(rows, D_RW)
    k = k_ref[...].reshape(rows, D_RW)
    v = v_ref[...].reshape(rows, D_RW)
    lo = lo_ref[...].reshape(rows, LORA_W)
    xw = lo[:, 0:PAIR]
    xa = lo[:, PAIR:2 * PAIR]
    dw = jnp.dot(jnp.tanh(xw).astype(jnp.bfloat16), w2_ref[0], preferred_element_type=f32)
    lw = -jnp.exp(-_softplus(-(w0_ref[0] + dw)) - 0.5)
    a = jax.nn.sigmoid(a0_ref[0] + jnp.dot(xa.astype(jnp.bfloat16), a2_ref[0], preferred_element_type=f32))
    kq = k * kk_ref[...]
    ones = _head_ones()
    sq = (kq * kq).astype(jnp.bfloat16)
    ss = jnp.concatenate([jnp.dot(sq[:, p * PAIR:(p + 1) * PAIR], ones, preferred_element_type=f32)
                          for p in range(D_RW // PAIR)], axis=1)
    kk = kq / jnp.maximum(jnp.sqrt(ss), 1e-12)
    kdir = k * (1.0 + (a - 1.0) * ka_ref[...])
    kka = kk * a
    n = D_RW // PAIR
    parts = [(bi, slice(bi * C, (bi + 1) * C), slice(p * PAIR, (p + 1) * PAIR)) for bi in range(nbat) for p in range(n)]
    pick = lambda x: [x[rs, ls] for _, rs, ls in parts]
    ys, hn = _wkv_chunk(pick(r), pick(lw), pick(kdir), pick(v), pick(kk), pick(kka),
                        [h_ref[i] for i in range(len(parts))], rev)
    for i, (bi, _, ls) in enumerate(parts):
        o_ref[0, bi, :, ls] = ys[i]
        h_ref[i] = hn[i]


def wkv7_fused(sc, rcol, lcol, w0, a0, w2p, a2p, k_k, k_a, nbat=2):
    B, L, _ = sc.shape
    C = WKV_CHUNK
    nc = L // C
    D = D_RW
    nbat = min(nbat, B)
    assert L % C == 0 and B % nbat == 0
    cm = lambda d, c: c + d * (nc - 1 - 2 * c)
    col = lambda j: pl.BlockSpec((nbat, C, D), lambda d, b, c: (b, cm(d, c), j))
    vec2 = pl.BlockSpec((1, 1, D), lambda d, b, c: (d, 0, 0))
    mat2 = pl.BlockSpec((1, PAIR, D), lambda d, b, c: (d, 0, 0))
    vec = pl.BlockSpec((1, D), lambda d, b, c: (0, 0))
    return pl.pallas_call(
        functools.partial(_wkv_fused_kernel, nbat=nbat),
        grid=(2, B // nbat, nc),
        in_specs=[col(rcol), col(rcol + 1), col(rcol + 2),
                  pl.BlockSpec((nbat, C, LORA_W), lambda d, b, c: (b, cm(d, c), lcol)),
                  vec2, vec2, mat2, mat2, vec, vec],
        out_specs=pl.BlockSpec((1, nbat, C, D), lambda d, b, c: (d, b, cm(d, c), 0)),
        out_shape=jax.ShapeDtypeStruct((2, B, L, D), jnp.float32),
        scratch_shapes=[pltpu.VMEM((nbat * (D // PAIR), PAIR, PAIR), jnp.float32)],
        compiler_params=pltpu.CompilerParams(dimension_semantics=("arbitrary",) * 3, vmem_limit_bytes=VMEM_LIMIT),
        name="wkv7",
    )(sc, sc, sc, sc, w0, a0, w2p, a2p, k_k, k_a)


def _rw_post_kernel(y_ref, r_ref, k_ref, v_ref, lo_ref, a0_ref, a2_ref, ka_ref, rk_ref, lnw_ref, lnb_ref, g2_ref, wo_ref, o_ref):
    f32 = jnp.float32
    ones = _head_ones()
    y = y_ref[0, 0] + y_ref[1, 0]
    mu = _group_sum(y, ones) * (1.0 / HEAD)
    yc = y - mu
    var = _group_sum(yc * yc, ones) * (1.0 / HEAD)
    out = yc * lax.rsqrt(var + RW_GN_EPS) * lnw_ref[...] + lnb_ref[...]
    r = r_ref[0]
    k = k_ref[0]
    v = v_ref[0]
    xa = lo_ref[0, :, PAIR:2 * PAIR].astype(jnp.bfloat16)
    for d in range(2):
        a = jax.nn.sigmoid(a0_ref[d] + jnp.dot(xa, a2_ref[d], preferred_element_type=f32))
        kdir = k * (1.0 + (a - 1.0) * ka_ref[...])
        out = out + _group_sum(r * kdir * rk_ref[d], ones) * v
    g = jnp.dot(jax.nn.sigmoid(lo_ref[0]).astype(jnp.bfloat16), g2_ref[...], preferred_element_type=f32)
    o_ref[0] = jnp.dot((out * g).astype(jnp.bfloat16), wo_ref[...], preferred_element_type=f32).astype(o_ref.dtype)


def rw_post(y2, sc, rcol, lcol, a0, a2p, k_a, r_k, ln_w, ln_b, g2p, w_o, tm=256):
    _, B, L, D = y2.shape
    DM = w_o.shape[1]
    tm = min(tm, L)
    col = lambda j: pl.BlockSpec((1, tm, D), lambda b, i: (b, i, j))
    full = lambda shape: pl.BlockSpec(shape, lambda b, i: (0,) * len(shape))
    return pl.pallas_call(
        _rw_post_kernel,
        grid=(B, L // tm),
        in_specs=[pl.BlockSpec((2, 1, tm, D), lambda b, i: (0, b, i, 0)),
                  col(rcol), col(rcol + 1), col(rcol + 2),
                  pl.BlockSpec((1, tm, LORA_W), lambda b, i: (b, i, lcol)),
                  full((2, 1, D)), full((2, PAIR, D)), full((1, D)), full((2, 1, D)), full((1, D)), full((1, D)),
                  full((LORA_W, D)), full((D, DM))],
        out_specs=pl.BlockSpec((1, tm, DM), lambda b, i: (b, i, 0)),
        out_shape=jax.ShapeDtypeStruct((B, L, DM), jnp.bfloat16),
        compiler_params=pltpu.CompilerParams(dimension_semantics=("arbitrary",) * 2, vmem_limit_bytes=VMEM_LIMIT),
        name="rw_post",
    )(y2, sc, sc, sc, sc, a0, a2p, k_a, r_k, ln_w, ln_b, g2p, w_o)


def rwkv_params(p):
    f32, bf = jnp.float32, jnp.bfloat16
    z64 = jnp.zeros((64, D_RW), f32)
    w2p = jnp.stack([jnp.concatenate([p['rw_w2'][0], z64]), jnp.concatenate([z64, p['rw_w2'][1]])]).astype(bf)
    a2p = jnp.stack([jnp.concatenate([p['rw_a2'][0], z64]), jnp.concatenate([z64, p['rw_a2'][1]])]).astype(bf)
    g2p = jnp.zeros((LORA_W, D_RW), f32).at[256:256 + 160].set(p['rw_g2']).astype(bf)
    return dict(w0=p['rw_w0'].reshape(2, 1, D_RW), a0=p['rw_a0'].reshape(2, 1, D_RW), w2p=w2p, a2p=a2p, g2p=g2p,
                k_k=p['rw_k_k'].reshape(1, D_RW), k_a=p['rw_k_a'].reshape(1, D_RW), r_k=p['rw_r_k'].reshape(2, 1, D_RW),
                ln_w=p['rw_ln_w'].reshape(1, D_RW), ln_b=p['rw_ln_b'].reshape(1, D_RW), w_o=p['rw_w_o'].astype(bf))


def rwkv(sc, rcol, lcol, q):
    y2 = wkv7_fused(sc, rcol, lcol, q['w0'], q['a0'], q['w2p'], q['a2p'], q['k_k'], q['k_a'])
    return rw_post(y2, sc, rcol, lcol, q['a0'], q['a2p'], q['k_a'], q['r_k'], q['ln_w'], q['ln_b'], q['g2p'], q['w_o'])


def _attn_kernel(q_ref, k_ref, v_ref, wo_ref, o_ref):
    hp = pl.program_id(2)

    @pl.when(hp == 0)
    def _():
        o_ref[...] = jnp.zeros_like(o_ref)

    outs = []
    for h in range(CB // CA_HEAD):
        sl = slice(h * CA_HEAD, (h + 1) * CA_HEAD)
        q = q_ref[0, :, sl].astype(jnp.bfloat16)
        s = lax.dot_general(q, k_ref[0, :, sl].astype(jnp.bfloat16), (((1,), (1,)), ((), ())),
                            preferred_element_type=jnp.float32) * (CA_HEAD ** -0.5)
        s = s - jnp.max(s, axis=-1, keepdims=True)
        e = jnp.exp(s)
        p = e / jnp.sum(e, axis=-1, keepdims=True)
        outs.append(jnp.dot(p.astype(jnp.bfloat16), v_ref[0, :, sl].astype(jnp.bfloat16), preferred_element_type=jnp.float32))
    att = jnp.concatenate(outs, axis=1).astype(jnp.bfloat16)
    o_ref[0] += jnp.dot(att, wo_ref[...], preferred_element_type=jnp.float32)


def mem_attention(proj, qcol, kv, w_o, tm=512):
    B, L, _ = proj.shape
    M = kv.shape[1]
    DM = w_o.shape[1]
    nhp = w_o.shape[0] // CB
    tm = min(tm, L)
    return pl.pallas_call(
        _attn_kernel,
        grid=(B, L // tm, nhp),
        in_specs=[pl.BlockSpec((1, tm, CB), lambda b, i, h: (b, i, qcol + h)),
                  pl.BlockSpec((1, M, CB), lambda b, i, h: (b, 0, h)),
                  pl.BlockSpec((1, M, CB), lambda b, i, h: (b, 0, nhp + h)),
                  pl.BlockSpec((CB, DM), lambda b, i, h: (h, 0))],
        out_specs=pl.BlockSpec((1, tm, DM), lambda b, i, h: (b, i, 0)),
        out_shape=jax.ShapeDtypeStruct((B, L, DM), jnp.float32),
        compiler_params=pltpu.CompilerParams(dimension_semantics=("arbitrary",) * 3, vmem_limit_bytes=VMEM_LIMIT),
        name="mem_attn",
    )(proj, kv, kv, w_o)


def _merge_kernel(hy_ref, rw_ref, ca_ref, g0_ref, g1_ref, g2_ref, wo_ref, x_ref, lg_ref, lb_ref, wr_ref,
                  o_ref, ob_ref, lo_ref):
    f32 = jnp.float32
    m = (jax.nn.sigmoid(g0_ref[0].astype(f32)) * hy_ref[0] + jax.nn.sigmoid(g1_ref[0].astype(f32)) * rw_ref[0]
         + jax.nn.sigmoid(g2_ref[0].astype(f32)) * ca_ref[0])
    h = DN_ALPHA * x_ref[0] + jnp.dot(m.astype(jnp.bfloat16), wo_ref[...], preferred_element_type=f32)
    mu = jnp.mean(h, axis=-1, keepdims=True)
    hc = h - mu
    var = jnp.mean(hc * hc, axis=-1, keepdims=True)
    y = hc * lax.rsqrt(var + LN_EPS) * lg_ref[...] + lb_ref[...]
    o_ref[0] = DN_ALPHA * y
    yb = y.astype(jnp.bfloat16)
    ob_ref[0] = yb
    lo_ref[0] = jnp.dot(yb, wr_ref[...], preferred_element_type=f32)


def merge_ln(hy, rw, ca, pq, gcol, w_out, x, ln_g, ln_b, w_router_pad, tm=256):
    B, L, DM = x.shape
    tm = min(tm, L)
    row = pl.BlockSpec((1, tm, DM), lambda b, i: (b, i, 0))
    gate = lambda g: pl.BlockSpec((1, tm, DM), lambda b, i: (b, i, gcol + g))
    vec = pl.BlockSpec((1, DM), lambda b, i: (0, 0))
    return pl.pallas_call(
        _merge_kernel,
        grid=(B, L // tm),
        in_specs=[row, row, row, gate(0), gate(1), gate(2),
                  pl.BlockSpec((DM, DM), lambda b, i: (0, 0)), row, vec, vec,
                  pl.BlockSpec((DM, LANES), lambda b, i: (0, 0))],
        out_specs=[row, row, pl.BlockSpec((1, tm, LANES), lambda b, i: (b, i, 0))],
        out_shape=[jax.ShapeDtypeStruct((B, L, DM), jnp.float32), jax.ShapeDtypeStruct((B, L, DM), jnp.bfloat16),
                   jax.ShapeDtypeStruct((B, L, LANES), jnp.float32)],
        compiler_params=pltpu.CompilerParams(dimension_semantics=("arbitrary",) * 2, vmem_limit_bytes=VMEM_LIMIT),
        name="merge_ln",
    )(hy, rw, ca, pq, pq, pq, w_out, x, ln_g, ln_b, w_router_pad)


COMBINE_ROWS = 1024
COMBINE_UNROLL = 16


def _row_copy(y_ref, buf, row, r, sem, to_hbm):
    hbm = y_ref.at[pl.ds(row, 1)]
    vm = buf.at[pl.ds(r, 1)]
    return pltpu.make_async_copy(vm, hbm, sem) if to_hbm else pltpu.make_async_copy(hbm, vm, sem)


def _combine_kernel(idx_ref, out_ref, yin_ref, y_ref, buf, sem_in, sem_out, *, R):
    del yin_ref

    def start_in(r, c):
        _row_copy(y_ref, buf, idx_ref[0, 0, r], r, sem_in, False).start()
        return c

    def wait_in(r, c):
        _row_copy(y_ref, buf, idx_ref[0, 0, r], r, sem_in, False).wait()
        return c

    def start_out(r, c):
        _row_copy(y_ref, buf, idx_ref[0, 0, r], r, sem_out, True).start()
        return c

    def wait_out(r, c):
        _row_copy(y_ref, buf, idx_ref[0, 0, r], r, sem_out, True).wait()
        return c

    lax.fori_loop(0, R, start_in, 0, unroll=COMBINE_UNROLL)
    lax.fori_loop(0, R, wait_in, 0, unroll=COMBINE_UNROLL)
    buf[...] = buf[...] + out_ref[0]
    lax.fori_loop(0, R, start_out, 0, unroll=COMBINE_UNROLL)
    lax.fori_loop(0, R, wait_out, 0, unroll=COMBINE_UNROLL)


def moe_combine(idx, out, y0, R=COMBINE_ROWS):
    E, C, D = out.shape
    R = min(R, C)
    assert C % R == 0 and R % COMBINE_UNROLL == 0
    nch = C // R
    return pl.pallas_call(
        functools.partial(_combine_kernel, R=R),
        grid=(E, nch),
        in_specs=[pl.BlockSpec((1, 1, R), lambda e, c: (e * nch + c, 0, 0), memory_space=pltpu.SMEM),
                  pl.BlockSpec((1, R, D), lambda e, c: (e, c, 0)),
                  pl.BlockSpec(memory_space=pl.ANY)],
        out_specs=pl.BlockSpec(memory_space=pl.ANY),
        out_shape=jax.ShapeDtypeStruct(y0.shape, y0.dtype),
        scratch_shapes=[pltpu.VMEM((R, D), jnp.float32), pltpu.SemaphoreType.DMA, pltpu.SemaphoreType.DMA],
        input_output_aliases={2: 0},
        compiler_params=pltpu.CompilerParams(dimension_semantics=("arbitrary", "arbitrary"), vmem_limit_bytes=VMEM_LIMIT,
                                             has_side_effects=True),
        name="moe_combine",
    )(idx.reshape(E * nch, 1, R), out, y0)


def _ln_kernel(x_ref, g_ref, b_ref, o_ref):
    h = x_ref[...]
    mu = jnp.mean(h, axis=-1, keepdims=True)
    hc = h - mu
    var = jnp.mean(hc * hc, axis=-1, keepdims=True)
    o_ref[...] = hc * lax.rsqrt(var + LN_EPS) * g_ref[...] + b_ref[...]


def layer_norm(x, g, b, tm=512):
    T, DM = x.shape
    row = pl.BlockSpec((tm, DM), lambda i: (i, 0))
    vec = pl.BlockSpec((1, DM), lambda i: (0, 0))
    return pl.pallas_call(
        _ln_kernel, grid=(T // tm,), in_specs=[row, vec, vec], out_specs=row,
        out_shape=jax.ShapeDtypeStruct((T, DM), jnp.float32),
        compiler_params=pltpu.CompilerParams(dimension_semantics=("arbitrary",), vmem_limit_bytes=VMEM_LIMIT),
        name="final_ln",
    )(x, g, b)


def _hyena_filters(L, w1, b1, w2, b2, w3, b3, w4, sin_freq):
    f32 = jnp.float32
    t = jnp.linspace(0.0, 1.0, L, dtype=f32)[:, None]
    ang = 2.0 * math.pi * jnp.arange(L, dtype=f32)[:, None] / L
    bands = jnp.linspace(1e-4, HY_BANDS - 1, HY_BANDS, dtype=f32)[None]
    z = jnp.concatenate([t, jnp.cos(bands * ang), -jnp.sin(bands * ang)], axis=-1)
    sf = sin_freq.astype(f32)
    h = jnp.sin(sf[0] * (z @ w1.astype(f32) + b1.astype(f32)))
    h = jnp.sin(sf[1] * (h @ w2.astype(f32) + b2.astype(f32)))
    h = jnp.sin(sf[2] * (h @ w3.astype(f32) + b3.astype(f32)))
    h = (h @ w4.astype(f32)).reshape(L, HY_ORDER, 2, D_HY)
    deltas = jnp.abs(jnp.linspace(math.log(HY_DECAY_TARGET) / HY_SLOW_DECAY_PCT,
                                  math.log(HY_DECAY_TARGET) / HY_FAST_DECAY_PCT, D_HY, dtype=f32))
    window = jnp.exp(-t * deltas[None])
    return h * window[:, None, None, :]


def _expert_choice_moe(resid, xb, logits, w_gate, w_up, w_down):
    T, D = xb.shape
    cap = (EC_CAPACITY * T) // N_EXPERTS
    aff = jax.nn.softmax(logits.astype(jnp.float32), axis=-1)
    gval, idx = lax.top_k(aff.T, cap)
    out = _moe_ffn(xb[idx], jnp.broadcast_to(gval[..., None], gval.shape + (LANES,)), w_gate, w_up, w_down)
    return moe_combine(idx, out, resid)


def _layer(x, mem, p):
    B, L, D = x.shape
    xb = x.astype(jnp.bfloat16)
    sc = proj_conv(xb, p['w_conv'], p['w_short_all'])
    pq = _mm3(xb, p['w_rest'], out_dtype=jnp.bfloat16)
    filt = _hyena_filters(L, p['hy_w1'], p['hy_b1'], p['hy_w2'], p['hy_b2'], p['hy_w3'], p['hy_b3'],
                          p['hy_w4'], p['hy_sin_freq'])
    hy = _mm3(hyena(sc, filt, p['hy_dbias'], _matmul), p['hy_w_o'], out_dtype=jnp.bfloat16)
    rw = rwkv(sc, P_RW // D_RW, P_LORA // LORA_W, p['rw'])
    ca = mem_attention(pq, N_BRANCH * D_MODEL // CB, _mm3(mem, p['ca_w_kv']), p['ca_w_o'])
    xs, x1b, logits = merge_ln(hy, rw, ca, pq, 0, p['w_out'], x, p['ln1_g'], p['ln1_b'], p['w_router_pad'])
    h = _expert_choice_moe(xs.reshape(B * L, D), x1b.reshape(B * L, D), logits.reshape(B * L, LANES)[:, :N_EXPERTS],
                           p['moe_w_gate'], p['moe_w_up'], p['moe_w_down'])
    return layer_norm(h, p['ln2_g'], p['ln2_b']).reshape(B, L, D)


def kernel(x_prompt, x_sample, mem_prompt, mem_sample, w_in, w_short, hy_w1, hy_b1, hy_w2, hy_b2, hy_w3, hy_b3, hy_w4, hy_sin_freq, hy_dbias, hy_w_o, rw_w0, rw_w2, rw_a0, rw_a2, rw_g2, rw_k_k, rw_k_a, rw_r_k, rw_ln_w, rw_ln_b, rw_w_o, ca_w_kv, ca_w_o, w_out, ln1_g, ln1_b, moe_w_router, moe_w_gate, moe_w_up, moe_w_down, ln2_g, ln2_b):
    params = dict(w_in=w_in, w_short=w_short, hy_w1=hy_w1, hy_b1=hy_b1, hy_w2=hy_w2, hy_b2=hy_b2,
                  hy_w3=hy_w3, hy_b3=hy_b3, hy_w4=hy_w4, hy_sin_freq=hy_sin_freq, hy_dbias=hy_dbias,
                  hy_w_o=hy_w_o, rw_w0=rw_w0, rw_w2=rw_w2, rw_a0=rw_a0, rw_a2=rw_a2, rw_g2=rw_g2,
                  rw_k_k=rw_k_k, rw_k_a=rw_k_a, rw_r_k=rw_r_k, rw_ln_w=rw_ln_w, rw_ln_b=rw_ln_b,
                  rw_w_o=rw_w_o, ca_w_kv=ca_w_kv, ca_w_o=ca_w_o, w_out=w_out, ln1_g=ln1_g, ln1_b=ln1_b,
                  moe_w_router=moe_w_router, moe_w_gate=moe_w_gate, moe_w_up=moe_w_up,
                  moe_w_down=moe_w_down, ln2_g=ln2_g, ln2_b=ln2_b)
    bf = jnp.bfloat16
    y_prompt, y_sample = x_prompt, x_sample
    for i in range(DEPTH):
        lp = {name: arr[i] for name, arr in params.items()}
        wi = lp['w_in']
        lp['w_conv'] = jnp.pad(wi[:, :OFF_Q], ((0, 0), (0, N_LORA_PAD - N_LORA))).astype(bf)
        lp['w_rest'] = jnp.concatenate([wi[:, OFF_GATE:], wi[:, OFF_Q:OFF_GATE]], axis=1).astype(bf)
        lp['w_short_all'] = jnp.pad(lp['w_short'], ((0, 0), (0, N_LORA_PAD - N_LORA)))
        for name in ('hy_w_o', 'ca_w_o', 'ca_w_kv', 'w_out'):
            lp[name] = lp[name].astype(bf)
        lp['rw'] = rwkv_params(lp)
        lp['w_router_pad'] = jnp.pad(lp['moe_w_router'], ((0, 0), (0, LANES - N_EXPERTS))).astype(bf)
        for name in ('ln1_g', 'ln1_b', 'ln2_g', 'ln2_b'):
            lp[name] = lp[name].reshape(1, D_MODEL)
        y_prompt = _layer(y_prompt, mem_prompt, lp)
        y_sample = _layer(y_sample, mem_sample, lp)
    return (y_prompt, y_sample)
```

```python
import functools
import math

import jax
import jax.numpy as jnp
from jax import lax
from jax.experimental import pallas as pl
from jax.experimental.pallas import tpu as pltpu

D_MODEL = 2048
DEPTH = 1
D_HY = D_MODEL // 2
HY_ORDER = 2
HY_EMB = 33
HY_BANDS = (HY_EMB - 1) // 2
HY_FFN = 64
HY_FAST_DECAY_PCT = 0.3
HY_SLOW_DECAY_PCT = 1.5
HY_DECAY_TARGET = 1e-2
D_RW = D_MODEL // 2
RW_HEAD = 64
RW_HEADS = D_RW // RW_HEAD
RW_DECAY_LORA = 64
RW_AAA_LORA = 64
RW_GATE_LORA = 160
RW_GN_EPS = 64e-5
D_CA = D_MODEL // 2
CA_HEADS = 4
CA_HEAD = D_CA // CA_HEADS
N_BRANCH = 3
N_EXPERTS = 16
EC_CAPACITY = 2
D_EXPERT = 2 * D_MODEL
LN_EPS = 1e-5
DN_ALPHA = (2 * DEPTH) ** 0.25

OFF_HY = 0
OFF_RW = OFF_HY + 3 * D_HY
OFF_LORA = OFF_RW + 3 * D_RW
N_LORA = 2 * RW_DECAY_LORA + 2 * RW_AAA_LORA + RW_GATE_LORA
OFF_Q = OFF_LORA + N_LORA
OFF_GATE = OFF_Q + D_CA
N_IN = OFF_GATE + N_BRANCH * D_MODEL

LANES = 128
N_LORA_PAD = -(-N_LORA // LANES) * LANES
P_HY = 0
P_RW = P_HY + 3 * D_HY
P_LORA = P_RW + 3 * D_RW
P_Q = P_LORA + N_LORA_PAD
P_GATE = P_Q + D_CA
P_ALL = P_GATE + N_BRANCH * D_MODEL

VMEM_LIMIT = 56 * 1024 * 1024
SUBLANES = 8
CB = 512
N_CONV = P_Q
LORA_W = N_LORA_PAD


def _mm_kernel(a_ref, b_ref, o_ref):
    o_ref[...] = jnp.dot(a_ref[...].astype(jnp.bfloat16), b_ref[...].astype(jnp.bfloat16),
                         preferred_element_type=jnp.float32).astype(o_ref.dtype)


def _matmul(a, b, tm=1024, tn=512, out_dtype=jnp.float32):
    M, K = a.shape
    N = b.shape[1]
    tm = min(tm, M)
    tn = min(tn, N)
    assert M % tm == 0 and N % tn == 0, (M, N, tm, tn)
    return pl.pallas_call(
        _mm_kernel,
        grid=(M // tm, N // tn),
        in_specs=[pl.BlockSpec((tm, K), lambda i, j: (i, 0)),
                  pl.BlockSpec((K, tn), lambda i, j: (0, j))],
        out_specs=pl.BlockSpec((tm, tn), lambda i, j: (i, j)),
        out_shape=jax.ShapeDtypeStruct((M, N), out_dtype),
        compiler_params=pltpu.CompilerParams(dimension_semantics=("arbitrary", "arbitrary"),
                                             vmem_limit_bytes=VMEM_LIMIT),
        name="mm",
    )(a, b)


def _mm3(x, w, **kw):
    B, L, K = x.shape
    return _matmul(x.reshape(B * L, K), w, **kw).reshape(B, L, w.shape[1])


def _moe_kernel(x_ref, gv_ref, wg_ref, wu_ref, wd_ref, o_ref):
    @pl.when(pl.program_id(2) == 0)
    def _():
        o_ref[...] = jnp.zeros_like(o_ref)

    x = x_ref[0]
    hg = jnp.dot(x, wg_ref[0].astype(jnp.bfloat16), preferred_element_type=jnp.float32)
    hu = jnp.dot(x, wu_ref[0].astype(jnp.bfloat16), preferred_element_type=jnp.float32)
    h = (hg * jax.nn.sigmoid(hg) * hu).astype(jnp.bfloat16)
    o_ref[0] += jnp.dot(h, wd_ref[0].astype(jnp.bfloat16), preferred_element_type=jnp.float32)

    @pl.when(pl.program_id(2) == pl.num_programs(2) - 1)
    def _():
        o_ref[0] = o_ref[0] * jnp.tile(gv_ref[0], (1, o_ref.shape[2] // LANES))


def _moe_ffn(xe, gv, w_gate, w_up, w_down, tm=1024, tf=256):
    E, C, D = xe.shape
    F = w_gate.shape[2]
    tm = min(tm, C)
    return pl.pallas_call(
        _moe_kernel,
        grid=(E, C // tm, F // tf),
        in_specs=[pl.BlockSpec((1, tm, D), lambda e, i, f: (e, i, 0)),
                  pl.BlockSpec((1, tm, LANES), lambda e, i, f: (e, i, 0)),
                  pl.BlockSpec((1, D, tf), lambda e, i, f: (e, 0, f)),
                  pl.BlockSpec((1, D, tf), lambda e, i, f: (e, 0, f)),
                  pl.BlockSpec((1, tf, D), lambda e, i, f: (e, f, 0))],
        out_specs=pl.BlockSpec((1, tm, D), lambda e, i, f: (e, i, 0)),
        out_shape=jax.ShapeDtypeStruct((E, C, D), jnp.float32),
        compiler_params=pltpu.CompilerParams(dimension_semantics=("arbitrary",) * 3, vmem_limit_bytes=VMEM_LIMIT),
        name="moe_ffn",
    )(xe, gv, w_gate, w_up, w_down)


WKV_CHUNK = 64
HEAD = RW_HEAD
PAIR = 2 * HEAD


def _bdot(a, b):
    return jnp.dot(a.astype(jnp.bfloat16), b.astype(jnp.bfloat16), preferred_element_type=jnp.float32)


def _bdot_nt(a, b):
    return lax.dot_general(a.astype(jnp.bfloat16), b.astype(jnp.bfloat16), (((1,), (1,)), ((), ())),
                           preferred_element_type=jnp.float32)


def _bdot_tn(a, b):
    return lax.dot_general(a.astype(jnp.bfloat16), b.astype(jnp.bfloat16), (((0,), (0,)), ((), ())),
                           preferred_element_type=jnp.float32)


def _split3(x):
    f32, bf = jnp.float32, jnp.bfloat16
    hi = x.astype(bf)
    r1 = x - hi.astype(f32)
    mid = r1.astype(bf)
    lo = (r1 - mid.astype(f32)).astype(bf)
    return hi, mid, lo


def _wkv_chunk(rs, lws, ks, vs, kks, kkas, ss, rev):
    C = WKV_CHUNK
    f32, bf = jnp.float32, jnp.bfloat16
    n = len(rs)
    ri = lax.broadcasted_iota(jnp.int32, (C, C), 0)
    ci = lax.broadcasted_iota(jnp.int32, (C, C), 1)
    dd = jnp.where(rev == 0, ri - ci, ci - ri)
    tri = (dd >= 0).astype(bf)
    lane = lax.broadcasted_iota(jnp.int32, (C, PAIR), 1)
    first = lane < HEAD
    r2 = lax.broadcasted_iota(jnp.int32, (2 * C, 2 * C), 0)
    c2 = lax.broadcasted_iota(jnp.int32, (2 * C, 2 * C), 1)
    d2 = jnp.where(rev == 0, r2 - c2, c2 - r2)
    strict = d2 > 0
    incl = d2 >= 0
    eye = (r2 == c2).astype(f32)

    def stack(x):
        return jnp.concatenate([jnp.where(first, x, 0.0), jnp.where(first, 0.0, x)], axis=0)

    cums = []
    for lw in lws:
        hi, mid, lo = _split3(lw)
        cums.append(jnp.dot(tri, hi, preferred_element_type=f32) + jnp.dot(tri, mid, preferred_element_type=f32)
                    + jnp.dot(tri, lo, preferred_element_type=f32))
    galls = [jnp.exp(jnp.sum(lw, axis=0, keepdims=True)) for lw in lws]
    lhs, rhs, vS, a2k2 = [], [], [], []
    for i in range(n):
        cum, lw = cums[i], lws[i]
        ginv = jnp.exp(-cum)
        bt = stack(kks[i] * jnp.exp(cum - lw))
        rt = stack(rs[i] * jnp.exp(cum))
        at = stack(-(kkas[i] * ginv))
        kt = stack(ks[i] * ginv)
        lhs.append(jnp.concatenate([bt, rt], axis=0).astype(bf))
        rhs.append(jnp.concatenate([at, kt], axis=0).astype(bf))
        a2k2.append((jnp.concatenate([at, kt], axis=0) * galls[i]).astype(bf))
        vS.append(stack(vs[i]).astype(bf))
    gram = [_bdot_nt(lhs[i], rhs[i]) for i in range(n)]
    C2 = 2 * C
    mab = [jnp.where(strict, g[:C2, :C2], 0.0) for g in gram]
    mbk = [jnp.where(strict, g[:C2, C2:], 0.0).astype(bf) for g in gram]
    mrr = [jnp.concatenate([jnp.where(incl, g[C2:, :C2], 0.0), jnp.where(incl, g[C2:, C2:], 0.0)], axis=1).astype(bf)
           for g in gram]
    ts = [eye + m for m in mab]
    ps = [_bdot(m, m) for m in mab]
    for _ in range(4):
        xs = [_bdot(jnp.concatenate([t, p], axis=0), p) for t, p in zip(ts, ps)]
        ts = [t + x[:C2] for t, x in zip(ts, xs)]
        ps = [x[C2:] for x in xs]
    ts = [t + _bdot(t, p) for t, p in zip(ts, ps)]
    sB = [s.astype(bf) for s in ss]
    hs = [_bdot_nt(lhs[i], sB[i]) for i in range(n)]
    wS = [hs[i][:C2] + _bdot(mbk[i], vS[i]) for i in range(n)]
    uS = [_bdot(ts[i], wS[i]).astype(bf) for i in range(n)]
    uv = [jnp.concatenate([uS[i], vS[i]], axis=0) for i in range(n)]
    yS = [hs[i][C2:] + _bdot(mrr[i], uv[i]) for i in range(n)]
    ys = [y[:C] + y[C:] for y in yS]
    sn = [ss[i] * galls[i] + _bdot_tn(uv[i], a2k2[i]) for i in range(n)]
    return ys, sn


HALO = 2 * SUBLANES


def _projconv_kernel(x_ref, h_ref, w_ref, ws_ref, o_ref):
    w = w_ref[...]
    p = jnp.dot(x_ref[...], w, preferred_element_type=jnp.float32)
    ph = jnp.dot(h_ref[0], w, preferred_element_type=jnp.float32)
    tm = p.shape[0]
    rid = lax.broadcasted_iota(jnp.int32, (tm, 1), 0)
    prev = jnp.where(rid == 0, ph[SUBLANES - 1:SUBLANES, :], pltpu.roll(p, 1, 0))
    nxt = jnp.where(rid == tm - 1, ph[SUBLANES:SUBLANES + 1, :], pltpu.roll(p, tm - 1, 0))
    o_ref[...] = ws_ref[0:1, :] * prev + ws_ref[1:2, :] * p + ws_ref[2:3, :] * nxt


def _halo_rows(xb, tm):
    B, L, K = xb.shape
    nt = L // tm
    xr = xb.reshape(B, nt, tm, K)
    zero = jnp.zeros((B, 1, SUBLANES, K), xb.dtype)
    above = jnp.concatenate([zero, xr[:, :-1, tm - SUBLANES:]], axis=1)
    below = jnp.concatenate([xr[:, 1:, :SUBLANES], zero], axis=1)
    return jnp.concatenate([above, below], axis=2).reshape(B * nt, HALO, K)


def proj_conv(xb, w, ws, tm=1024, tn=512):
    B, L, K = xb.shape
    N = w.shape[1]
    tm = min(tm, L)
    assert L % tm == 0 and N % tn == 0
    halo = _halo_rows(xb, tm)
    T = B * L
    out = pl.pallas_call(
        _projconv_kernel,
        grid=(T // tm, N // tn),
        in_specs=[pl.BlockSpec((tm, K), lambda i, j: (i, 0)),
                  pl.BlockSpec((1, HALO, K), lambda i, j: (i, 0, 0)),
                  pl.BlockSpec((K, tn), lambda i, j: (0, j)),
                  pl.BlockSpec((3, tn), lambda i, j: (0, j))],
        out_specs=pl.BlockSpec((tm, tn), lambda i, j: (i, j)),
        out_shape=jax.ShapeDtypeStruct((T, N), jnp.float32),
        compiler_params=pltpu.CompilerParams(dimension_semantics=("arbitrary", "arbitrary"), vmem_limit_bytes=VMEM_LIMIT),
        name="proj_conv",
    )(xb.reshape(T, K), halo, w, ws)
    return out.reshape(B, L, N)


def _dft_mats(L):
    N = 2 * L
    f = lax.broadcasted_iota(jnp.int32, (L, L), 0)
    s = lax.broadcasted_iota(jnp.int32, (L, L), 1)
    ang = ((f * s) % N).astype(jnp.float32) * (2.0 * math.pi / N)
    a = jnp.cos(ang)
    sn = jnp.sin(ang)
    alt_s = (1 - 2 * (s % 2)).astype(jnp.float32)
    alt_f = (1 - 2 * (f % 2)).astype(jnp.float32)
    b = jnp.where(f == 0, alt_s, sn)
    b2 = jnp.where(s == 0, alt_f, sn)
    return a.astype(jnp.bfloat16), b.astype(jnp.bfloat16), b2.astype(jnp.bfloat16)


HY_BLOCK = 1024


def _hy_fwd_kernel(z_ref, a_ref, b_ref, kp_ref, kq_ref, yp_ref, yq_ref, zb_ref, *, tf, nb, P):
    j = pl.program_id(2)

    @pl.when(j == 0)
    def _():
        zb_ref[...] = z_ref[0].astype(jnp.bfloat16)

    a = a_ref[...]
    b = b_ref[...]
    ps = [jnp.dot(a, zb_ref[J * P:(J + 1) * P, :], preferred_element_type=jnp.float32) for J in range(nb)]
    qs = [jnp.dot(b, zb_ref[J * P:(J + 1) * P, :], preferred_element_type=jnp.float32) for J in range(nb)]
    row = lax.broadcasted_iota(jnp.int32, ps[0].shape, 0) + j * tf
    packed = row == 0
    for I in range(nb):
        yp = None
        yq = None
        for J in range(nb):
            kp = kp_ref[I - J + nb - 1]
            kq = kq_ref[I - J + nb - 1]
            kqq = kq * qs[J]
            tp = kp * ps[J] - jnp.where(packed, 0.0, kqq)
            tq = jnp.where(packed, kqq, kp * qs[J] + kq * ps[J])
            yp = tp if yp is None else yp + tp
            yq = tq if yq is None else yq + tq
        yp_ref[0, I] = yp.astype(yp_ref.dtype)
        yq_ref[0, I] = yq.astype(yq_ref.dtype)


def _hy_fwd(z, zcol, a, b, kp, kq, tf=256, tc=512):
    B, L, _ = z.shape
    nk, P, C = kp.shape
    nb = (nk + 1) // 2
    assert nb * P == L
    tf = min(tf, P)
    zspec = pl.BlockSpec((1, L, tc), lambda b_, c, j: (b_, 0, zcol + c))
    mspec = pl.BlockSpec((tf, P), lambda b_, c, j: (j, 0))
    kspec = pl.BlockSpec((nk, tf, tc), lambda b_, c, j: (0, j, c))
    ospec = pl.BlockSpec((1, nb, tf, tc), lambda b_, c, j: (b_, 0, j, c))
    return pl.pallas_call(
        functools.partial(_hy_fwd_kernel, tf=tf, nb=nb, P=P),
        grid=(B, C // tc, P // tf),
        in_specs=[zspec, mspec, mspec, kspec, kspec],
        out_specs=[ospec, ospec],
        out_shape=[jax.ShapeDtypeStruct((B, nb, P, C), jnp.bfloat16)] * 2,
        scratch_shapes=[pltpu.VMEM((L, tc), jnp.bfloat16)],
        compiler_params=pltpu.CompilerParams(dimension_semantics=("arbitrary",) * 3, vmem_limit_bytes=VMEM_LIMIT),
        name="hy_fwd",
    )(z, a, b, kp, kq)


def _hy_inv_kernel(yp_ref, yq_ref, a_ref, b2_ref, z_ref, x_ref, db_ref, o_ref):
    y = jnp.dot(a_ref[...], yp_ref[0, 0], preferred_element_type=jnp.float32)
    y += jnp.dot(b2_ref[...], yq_ref[0, 0], preferred_element_type=jnp.float32)
    o_ref[0] = (x_ref[0] * (y + z_ref[0] * db_ref[...])).astype(o_ref.dtype)


def _hy_inv(yp, yq, a, b2, zarr, zcol, xarr, xcol, db, out_dtype, tt=512, tc=512):
    B, nb, P, C = yp.shape
    tt = min(tt, P)
    nt = P // tt
    yspec = pl.BlockSpec((1, 1, P, tc), lambda b_, c, I, i: (b_, I, 0, c))
    mspec = pl.BlockSpec((tt, P), lambda b_, c, I, i: (i, 0))
    return pl.pallas_call(
        _hy_inv_kernel,
        grid=(B, C // tc, nb, nt),
        in_specs=[yspec, yspec, mspec, mspec,
                  pl.BlockSpec((1, tt, tc), lambda b_, c, I, i: (b_, I * nt + i, zcol + c)),
                  pl.BlockSpec((1, tt, tc), lambda b_, c, I, i: (b_, I * nt + i, xcol + c)),
                  pl.BlockSpec((1, tc), lambda b_, c, I, i: (0, c))],
        out_specs=pl.BlockSpec((1, tt, tc), lambda b_, c, I, i: (b_, I * nt + i, c)),
        out_shape=jax.ShapeDtypeStruct((B, nb * P, C), out_dtype),
        compiler_params=pltpu.CompilerParams(dimension_semantics=("arbitrary",) * 4, vmem_limit_bytes=VMEM_LIMIT),
        name="hy_inv",
    )(yp, yq, a, b2, zarr, xarr, db)


def _hy_spectra(filt, a, b, P, mm):
    L, n_ord, _, C = filt.shape
    nb = L // P
    N = 2 * P
    assert P % 2 == 0
    hcat = jnp.concatenate([filt[:, o, s].reshape(nb, P, C) for o in range(n_ord) for s in range(2)], axis=0)
    ns = hcat.shape[0]
    flat = jnp.transpose(hcat, (1, 0, 2)).reshape(P, ns * C).astype(jnp.bfloat16)
    cs = mm(a, flat).reshape(P, ns, C)
    sn = mm(b, flat).reshape(P, ns, C)
    row0 = (jnp.arange(P) == 0)[:, None]
    sign = (1 - 2 * (jnp.arange(P) % 2)).astype(jnp.float32)[:, None]
    wgt = jnp.where(row0, 1.0 / N, 2.0 / N)
    zero = jnp.zeros((P, C), jnp.float32)
    out = []
    for o in range(n_ord):
        def fwd(s, j, drop_first):
            if j >= nb:
                return zero, zero
            k = (o * 2 + s) * nb + j
            x0 = hcat[k, 0:1, :].astype(jnp.bfloat16).astype(jnp.float32) if drop_first else 0.0
            return cs[:, k] - x0, sn[:, k] - jnp.where(row0, x0, 0.0)

        def bwd(s, j):
            c, q = fwd(s, j, True)
            return sign * c, jnp.where(row0, q, -sign * q)

        def first(s, j):
            if j >= nb:
                return zero, zero
            k = (o * 2 + s) * nb + j
            x0 = jnp.broadcast_to(hcat[k, 0:1, :].astype(jnp.bfloat16).astype(jnp.float32), (P, C))
            return x0, jnp.where(row0, x0, 0.0)

        kps, kqs = [], []
        for e in range(-(nb - 1), nb):
            if e >= 1:
                cp, qp = fwd(0, e, False)
                cn, qn = bwd(0, e - 1)
            elif e == 0:
                cp, qp = fwd(0, 0, False)
                cn, qn = fwd(1, 0, True)
            else:
                g = -e
                c1, q1 = bwd(1, g - 1)
                c2, q2 = first(1, g)
                cp, qp = c1 + c2, q1 + q2
                cn, qn = fwd(1, g, True)
            kps.append((cp + cn) * wgt)
            kqs.append(jnp.where(row0, qp + qn, qp - qn) * wgt)
        out.append((jnp.stack(kps), jnp.stack(kqs)))
    return out


def hyena(sc, filt, dbias, mm, P=HY_BLOCK, tc=512):
    L = sc.shape[1]
    C = filt.shape[3]
    P = min(P, L)
    a, b, b2 = _dft_mats(P)
    (kp1, kq1), (kp2, kq2) = _hy_spectra(filt, a, b, P, mm)
    db = dbias.astype(jnp.float32)
    nbc = C // tc
    yp, yq = _hy_fwd(sc, 0, a, b, kp1, kq1, tc=tc)
    z2 = _hy_inv(yp, yq, a, b2, sc, 0, sc, nbc, db[0:1], jnp.float32, tc=tc)
    yp, yq = _hy_fwd(z2, 0, a, b, kp2, kq2, tc=tc)
    return _hy_inv(yp, yq, a, b2, z2, 0, sc, 2 * nbc, db[1:2], jnp.bfloat16, tc=tc)


def _split_dot(x, ones_bf16):
    hi = x.astype(jnp.bfloat16)
    lo = (x - hi.astype(jnp.float32)).astype(jnp.bfloat16)
    return (jnp.dot(hi, ones_bf16, preferred_element_type=jnp.float32)
            + jnp.dot(lo, ones_bf16, preferred_element_type=jnp.float32))


def _head_ones():
    r = lax.broadcasted_iota(jnp.int32, (PAIR, PAIR), 0) // HEAD
    c = lax.broadcasted_iota(jnp.int32, (PAIR, PAIR), 1) // HEAD
    return (r == c).astype(jnp.bfloat16)


def _group_sum(x, ones):
    return jnp.concatenate([_split_dot(x[:, p * PAIR:(p + 1) * PAIR], ones) for p in range(x.shape[1] // PAIR)], axis=1)


def _softplus(x):
    return jnp.maximum(x, 0.0) + jnp.log1p(jnp.exp(-jnp.abs(x)))


def _wkv_fused_kernel(r_ref, k_ref, v_ref, lo_ref, w0_ref, a0_ref, w2_ref, a2_ref, kk_ref, ka_ref, o_ref, h_ref, *, nbat):
    rev = pl.program_id(0)

    @pl.when(pl.program_id(2) == 0)
    def _():
        h_ref[...] = jnp.zeros_like(h_ref)

    f32 = jnp.float32
    C = WKV_CHUNK
    rows = nbat * C
    r = r_ref[...].reshape(rows, D_RW)
    k = k_ref[...].reshape(rows, D_RW)
    v = v_ref[...].reshape(rows, D_RW)
    lo = lo_ref[...].reshape(rows, LORA_W)
    xw = lo[:, 0:PAIR]
    xa = lo[:, PAIR:2 * PAIR]
    dw = jnp.dot(jnp.tanh(xw).astype(jnp.bfloat16), w2_ref[0], preferred_element_type=f32)
    lw = -jnp.exp(-_softplus(-(w0_ref[0] + dw)) - 0.5)
    a = jax.nn.sigmoid(a0_ref[0] + jnp.dot(xa.astype(jnp.bfloat16), a2_ref[0], preferred_element_type=f32))
    kq = k * kk_ref[...]
    ones = _head_ones()
    sq = (kq * kq).astype(jnp.bfloat16)
    ss = jnp.concatenate([jnp.dot(sq[:, p * PAIR:(p + 1) * PAIR], ones, preferred_element_type=f32)
                          for p in range(D_RW // PAIR)], axis=1)
    kk = kq / jnp.maximum(jnp.sqrt(ss), 1e-12)
    kdir = k * (1.0 + (a - 1.0) * ka_ref[...])
    kka = kk * a
    n = D_RW // PAIR
    parts = [(bi, slice(bi * C, (bi + 1) * C), slice(p * PAIR, (p + 1) * PAIR)) for bi in range(nbat) for p in range(n)]
    pick = lambda x: [x[rs, ls] for _, rs, ls in parts]
    ys, hn = _wkv_chunk(pick(r), pick(lw), pick(kdir), pick(v), pick(kk), pick(kka),
                        [h_ref[i] for i in range(len(parts))], rev)
    for i, (bi, _, ls) in enumerate(parts):
        o_ref[0, bi, :, ls] = ys[i]
        h_ref[i] = hn[i]


def wkv7_fused(sc, rcol, lcol, w0, a0, w2p, a2p, k_k, k_a, nbat=4):
    B, L, _ = sc.shape
    C = WKV_CHUNK
    nc = L // C
    D = D_RW
    nbat = min(nbat, B)
    assert L % C == 0 and B % nbat == 0
    cm = lambda d, c: c + d * (nc - 1 - 2 * c)
    col = lambda j: pl.BlockSpec((nbat, C, D), lambda d, b, c: (b, cm(d, c), j))
    vec2 = pl.BlockSpec((1, 1, D), lambda d, b, c: (d, 0, 0))
    mat2 = pl.BlockSpec((1, PAIR, D), lambda d, b, c: (d, 0, 0))
    vec = pl.BlockSpec((1, D), lambda d, b, c: (0, 0))
    return pl.pallas_call(
        functools.partial(_wkv_fused_kernel, nbat=nbat),
        grid=(2, B // nbat, nc),
        in_specs=[col(rcol), col(rcol + 1), col(rcol + 2),
                  pl.BlockSpec((nbat, C, LORA_W), lambda d, b, c: (b, cm(d, c), lcol)),
                  vec2, vec2, mat2, mat2, vec, vec],
        out_specs=pl.BlockSpec((1, nbat, C, D), lambda d, b, c: (d, b, cm(d, c), 0)),
        out_shape=jax.ShapeDtypeStruct((2, B, L, D), jnp.float32),
        scratch_shapes=[pltpu.VMEM((nbat * (D // PAIR), PAIR, PAIR), jnp.float32)],
        compiler_params=pltpu.CompilerParams(dimension_semantics=("arbitrary",) * 3, vmem_limit_bytes=VMEM_LIMIT),
        name="wkv7",
    )(sc, sc, sc, sc, w0, a0, w2p, a2p, k_k, k_a)


def _rw_post_kernel(y_ref, r_ref, k_ref, v_ref, lo_ref, a0_ref, a2_ref, ka_ref, rk_ref, lnw_ref, lnb_ref, g2_ref, wo_ref, o_ref):
    f32 = jnp.float32
    ones = _head_ones()
    y = y_ref[0, 0] + y_ref[1, 0]
    mu = _group_sum(y, ones) * (1.0 / HEAD)
    yc = y - mu
    var = _group_sum(yc * yc, ones) * (1.0 / HEAD)
    out = yc * lax.rsqrt(var + RW_GN_EPS) * lnw_ref[...] + lnb_ref[...]
    r = r_ref[0]
    k = k_ref[0]
    v = v_ref[0]
    xa = lo_ref[0, :, PAIR:2 * PAIR].astype(jnp.bfloat16)
    for d in range(2):
        a = jax.nn.sigmoid(a0_ref[d] + jnp.dot(xa, a2_ref[d], preferred_element_type=f32))
        kdir = k * (1.0 + (a - 1.0) * ka_ref[...])
        out = out + _group_sum(r * kdir * rk_ref[d], ones) * v
    g = jnp.dot(jax.nn.sigmoid(lo_ref[0]).astype(jnp.bfloat16), g2_ref[...], preferred_element_type=f32)
    o_ref[0] = jnp.dot((out * g).astype(jnp.bfloat16), wo_ref[...], preferred_element_type=f32).astype(o_ref.dtype)


def rw_post(y2, sc, rcol, lcol, a0, a2p, k_a, r_k, ln_w, ln_b, g2p, w_o, tm=256):
    _, B, L, D = y2.shape
    DM = w_o.shape[1]
    tm = min(tm, L)
    col = lambda j: pl.BlockSpec((1, tm, D), lambda b, i: (b, i, j))
    full = lambda shape: pl.BlockSpec(shape, lambda b, i: (0,) * len(shape))
    return pl.pallas_call(
        _rw_post_kernel,
        grid=(B, L // tm),
        in_specs=[pl.BlockSpec((2, 1, tm, D), lambda b, i: (0, b, i, 0)),
                  col(rcol), col(rcol + 1), col(rcol + 2),
                  pl.BlockSpec((1, tm, LORA_W), lambda b, i: (b, i, lcol)),
                  full((2, 1, D)), full((2, PAIR, D)), full((1, D)), full((2, 1, D)), full((1, D)), full((1, D)),
                  full((LORA_W, D)), full((D, DM))],
        out_specs=pl.BlockSpec((1, tm, DM), lambda b, i: (b, i, 0)),
        out_shape=jax.ShapeDtypeStruct((B, L, DM), jnp.bfloat16),
        compiler_params=pltpu.CompilerParams(dimension_semantics=("arbitrary",) * 2, vmem_limit_bytes=VMEM_LIMIT),
        name="rw_post",
    )(y2, sc, sc, sc, sc, a0, a2p, k_a, r_k, ln_w, ln_b, g2p, w_o)


def rwkv_params(p):
    f32, bf = jnp.float32, jnp.bfloat16
    z64 = jnp.zeros((64, D_RW), f32)
    w2p = jnp.stack([jnp.concatenate([p['rw_w2'][0], z64]), jnp.concatenate([z64, p['rw_w2'][1]])]).astype(bf)
    a2p = jnp.stack([jnp.concatenate([p['rw_a2'][0], z64]), jnp.concatenate([z64, p['rw_a2'][1]])]).astype(bf)
    g2p = jnp.zeros((LORA_W, D_RW), f32).at[256:256 + 160].set(p['rw_g2']).astype(bf)
    return dict(w0=p['rw_w0'].reshape(2, 1, D_RW), a0=p['rw_a0'].reshape(2, 1, D_RW), w2p=w2p, a2p=a2p, g2p=g2p,
                k_k=p['rw_k_k'].reshape(1, D_RW), k_a=p['rw_k_a'].reshape(1, D_RW), r_k=p['rw_r_k'].reshape(2, 1, D_RW),
                ln_w=p['rw_ln_w'].reshape(1, D_RW), ln_b=p['rw_ln_b'].reshape(1, D_RW), w_o=p['rw_w_o'].astype(bf))


def rwkv(sc, rcol, lcol, q):
    y2 = wkv7_fused(sc, rcol, lcol, q['w0'], q['a0'], q['w2p'], q['a2p'], q['k_k'], q['k_a'])
    return rw_post(y2, sc, rcol, lcol, q['a0'], q['a2p'], q['k_a'], q['r_k'], q['ln_w'], q['ln_b'], q['g2p'], q['w_o'])


def _attn_kernel(q_ref, k_ref, v_ref, wo_ref, o_ref):
    hp = pl.program_id(2)

    @pl.when(hp == 0)
    def _():
        o_ref[...] = jnp.zeros_like(o_ref)

    outs = []
    for h in range(CB // CA_HEAD):
        sl = slice(h * CA_HEAD, (h + 1) * CA_HEAD)
        q = q_ref[0, :, sl].astype(jnp.bfloat16)
        s = lax.dot_general(q, k_ref[0, :, sl].astype(jnp.bfloat16), (((1,), (1,)), ((), ())),
                            preferred_element_type=jnp.float32) * (CA_HEAD ** -0.5)
        s = s - jnp.max(s, axis=-1, keepdims=True)
        e = jnp.exp(s)
        p = e / jnp.sum(e, axis=-1, keepdims=True)
        outs.append(jnp.dot(p.astype(jnp.bfloat16), v_ref[0, :, sl].astype(jnp.bfloat16), preferred_element_type=jnp.float32))
    att = jnp.concatenate(outs, axis=1).astype(jnp.bfloat16)
    o_ref[0] += jnp.dot(att, wo_ref[...], preferred_element_type=jnp.float32)


def mem_attention(proj, qcol, kv, w_o, tm=512):
    B, L, _ = proj.shape
    M = kv.shape[1]
    DM = w_o.shape[1]
    nhp = w_o.shape[0] // CB
    tm = min(tm, L)
    return pl.pallas_call(
        _attn_kernel,
        grid=(B, L // tm, nhp),
        in_specs=[pl.BlockSpec((1, tm, CB), lambda b, i, h: (b, i, qcol + h)),
                  pl.BlockSpec((1, M, CB), lambda b, i, h: (b, 0, h)),
                  pl.BlockSpec((1, M, CB), lambda b, i, h: (b, 0, nhp + h)),
                  pl.BlockSpec((CB, DM), lambda b, i, h: (h, 0))],
        out_specs=pl.BlockSpec((1, tm, DM), lambda b, i, h: (b, i, 0)),
        out_shape=jax.ShapeDtypeStruct((B, L, DM), jnp.float32),
        compiler_params=pltpu.CompilerParams(dimension_semantics=("arbitrary",) * 3, vmem_limit_bytes=VMEM_LIMIT),
        name="mem_attn",
    )(proj, kv, kv, w_o)


def _merge_kernel(hy_ref, rw_ref, ca_ref, g0_ref, g1_ref, g2_ref, wo_ref, x_ref, lg_ref, lb_ref, wr_ref,
                  o_ref, ob_ref, lo_ref):
    f32 = jnp.float32
    m = (jax.nn.sigmoid(g0_ref[0].astype(f32)) * hy_ref[0] + jax.nn.sigmoid(g1_ref[0].astype(f32)) * rw_ref[0]
         + jax.nn.sigmoid(g2_ref[0].astype(f32)) * ca_ref[0])
    h = DN_ALPHA * x_ref[0] + jnp.dot(m.astype(jnp.bfloat16), wo_ref[...], preferred_element_type=f32)
    mu = jnp.mean(h, axis=-1, keepdims=True)
    hc = h - mu
    var = jnp.mean(hc * hc, axis=-1, keepdims=True)
    y = hc * lax.rsqrt(var + LN_EPS) * lg_ref[...] + lb_ref[...]
    o_ref[0] = DN_ALPHA * y
    yb = y.astype(jnp.bfloat16)
    ob_ref[0] = yb
    lo_ref[0] = jnp.dot(yb, wr_ref[...], preferred_element_type=f32)


def merge_ln(hy, rw, ca, pq, gcol, w_out, x, ln_g, ln_b, w_router_pad, tm=256):
    B, L, DM = x.shape
    tm = min(tm, L)
    row = pl.BlockSpec((1, tm, DM), lambda b, i: (b, i, 0))
    gate = lambda g: pl.BlockSpec((1, tm, DM), lambda b, i: (b, i, gcol + g))
    vec = pl.BlockSpec((1, DM), lambda b, i: (0, 0))
    return pl.pallas_call(
        _merge_kernel,
        grid=(B, L // tm),
        in_specs=[row, row, row, gate(0), gate(1), gate(2),
                  pl.BlockSpec((DM, DM), lambda b, i: (0, 0)), row, vec, vec,
                  pl.BlockSpec((DM, LANES), lambda b, i: (0, 0))],
        out_specs=[row, row, pl.BlockSpec((1, tm, LANES), lambda b, i: (b, i, 0))],
        out_shape=[jax.ShapeDtypeStruct((B, L, DM), jnp.float32), jax.ShapeDtypeStruct((B, L, DM), jnp.bfloat16),
                   jax.ShapeDtypeStruct((B, L, LANES), jnp.float32)],
        compiler_params=pltpu.CompilerParams(dimension_semantics=("arbitrary",) * 2, vmem_limit_bytes=VMEM_LIMIT),
        name="merge_ln",
    )(hy, rw, ca, pq, pq, pq, w_out, x, ln_g, ln_b, w_router_pad)


COMBINE_ROWS = 1024
COMBINE_UNROLL = 16


def _row_copy(y_ref, buf, row, r, sem, to_hbm):
    hbm = y_ref.at[pl.ds(row, 1)]
    vm = buf.at[pl.ds(r, 1)]
    return pltpu.make_async_copy(vm, hbm, sem) if to_hbm else pltpu.make_async_copy(hbm, vm, sem)


def _combine_kernel(idx_ref, out_ref, yin_ref, y_ref, buf, sem_in, sem_out, *, R):
    del yin_ref

    def start_in(r, c):
        _row_copy(y_ref, buf, idx_ref[0, 0, r], r, sem_in, False).start()
        return c

    def wait_in(r, c):
        _row_copy(y_ref, buf, idx_ref[0, 0, r], r, sem_in, False).wait()
        return c

    def start_out(r, c):
        _row_copy(y_ref, buf, idx_ref[0, 0, r], r, sem_out, True).start()
        return c

    def wait_out(r, c):
        _row_copy(y_ref, buf, idx_ref[0, 0, r], r, sem_out, True).wait()
        return c

    lax.fori_loop(0, R, start_in, 0, unroll=COMBINE_UNROLL)
    lax.fori_loop(0, R, wait_in, 0, unroll=COMBINE_UNROLL)
    buf[...] = buf[...] + out_ref[0]
    lax.fori_loop(0, R, start_out, 0, unroll=COMBINE_UNROLL)
    lax.fori_loop(0, R, wait_out, 0, unroll=COMBINE_UNROLL)


def moe_combine(idx, out, y0, R=COMBINE_ROWS):
    E, C, D = out.shape
    R = min(R, C)
    assert C % R == 0 and R % COMBINE_UNROLL == 0
    nch = C // R
    return pl.pallas_call(
        functools.partial(_combine_kernel, R=R),
        grid=(E, nch),
        in_specs=[pl.BlockSpec((1, 1, R), lambda e, c: (e * nch + c, 0, 0), memory_space=pltpu.SMEM),
                  pl.BlockSpec((1, R, D), lambda e, c: (e, c, 0)),
                  pl.BlockSpec(memory_space=pl.ANY)],
        out_specs=pl.BlockSpec(memory_space=pl.ANY),
        out_shape=jax.ShapeDtypeStruct(y0.shape, y0.dtype),
        scratch_shapes=[pltpu.VMEM((R, D), jnp.float32), pltpu.SemaphoreType.DMA, pltpu.SemaphoreType.DMA],
        input_output_aliases={2: 0},
        compiler_params=pltpu.CompilerParams(dimension_semantics=("arbitrary", "arbitrary"), vmem_limit_bytes=VMEM_LIMIT,
                                             has_side_effects=True),
        name="moe_combine",
    )(idx.reshape(E * nch, 1, R), out, y0)


def _ln_kernel(x_ref, g_ref, b_ref, o_ref):
    h = x_ref[...]
    mu = jnp.mean(h, axis=-1, keepdims=True)
    hc = h - mu
    var = jnp.mean(hc * hc, axis=-1, keepdims=True)
    o_ref[...] = hc * lax.rsqrt(var + LN_EPS) * g_ref[...] + b_ref[...]


def layer_norm(x, g, b, tm=512):
    T, DM = x.shape
    row = pl.BlockSpec((tm, DM), lambda i: (i, 0))
    vec = pl.BlockSpec((1, DM), lambda i: (0, 0))
    return pl.pallas_call(
        _ln_kernel, grid=(T // tm,), in_specs=[row, vec, vec], out_specs=row,
        out_shape=jax.ShapeDtypeStruct((T, DM), jnp.float32),
        compiler_params=pltpu.CompilerParams(dimension_semantics=("arbitrary",), vmem_limit_bytes=VMEM_LIMIT),
        name="final_ln",
    )(x, g, b)


def _hyena_filters(L, w1, b1, w2, b2, w3, b3, w4, sin_freq):
    f32 = jnp.float32
    t = jnp.linspace(0.0, 1.0, L, dtype=f32)[:, None]
    ang = 2.0 * math.pi * jnp.arange(L, dtype=f32)[:, None] / L
    bands = jnp.linspace(1e-4, HY_BANDS - 1, HY_BANDS, dtype=f32)[None]
    z = jnp.concatenate([t, jnp.cos(bands * ang), -jnp.sin(bands * ang)], axis=-1)
    sf = sin_freq.astype(f32)
    h = jnp.sin(sf[0] * (z @ w1.astype(f32) + b1.astype(f32)))
    h = jnp.sin(sf[1] * (h @ w2.astype(f32) + b2.astype(f32)))
    h = jnp.sin(sf[2] * (h @ w3.astype(f32) + b3.astype(f32)))
    h = (h @ w4.astype(f32)).reshape(L, HY_ORDER, 2, D_HY)
    deltas = jnp.abs(jnp.linspace(math.log(HY_DECAY_TARGET) / HY_SLOW_DECAY_PCT,
                                  math.log(HY_DECAY_TARGET) / HY_FAST_DECAY_PCT, D_HY, dtype=f32))
    window = jnp.exp(-t * deltas[None])
    return h * window[:, None, None, :]


def _expert_choice_moe(resid, xb, logits, w_gate, w_up, w_down):
    T, D = xb.shape
    cap = (EC_CAPACITY * T) // N_EXPERTS
    aff = jax.nn.softmax(logits.astype(jnp.float32), axis=-1)
    gval, idx = lax.top_k(aff.T, cap)
    out = _moe_ffn(xb[idx], jnp.broadcast_to(gval[..., None], gval.shape + (LANES,)), w_gate, w_up, w_down)
    return moe_combine(idx, out, resid)


def _layer(x, mem, p):
    B, L, D = x.shape
    xb = x.astype(jnp.bfloat16)
    sc = proj_conv(xb, p['w_conv'], p['w_short_all'])
    pq = _mm3(xb, p['w_rest'], out_dtype=jnp.bfloat16)
    filt = _hyena_filters(L, p['hy_w1'], p['hy_b1'], p['hy_w2'], p['hy_b2'], p['hy_w3'], p['hy_b3'],
                          p['hy_w4'], p['hy_sin_freq'])
    hy = _mm3(hyena(sc, filt, p['hy_dbias'], _matmul), p['hy_w_o'], out_dtype=jnp.bfloat16)
    rw = rwkv(sc, P_RW // D_RW, P_LORA // LORA_W, p['rw'])
    ca = mem_attention(pq, N_BRANCH * D_MODEL // CB, _mm3(mem, p['ca_w_kv']), p['ca_w_o'])
    xs, x1b, logits = merge_ln(hy, rw, ca, pq, 0, p['w_out'], x, p['ln1_g'], p['ln1_b'], p['w_router_pad'])
    h = _expert_choice_moe(xs.reshape(B * L, D), x1b.reshape(B * L, D), logits.reshape(B * L, LANES)[:, :N_EXPERTS],
                           p['moe_w_gate'], p['moe_w_up'], p['moe_w_down'])
    return layer_norm(h, p['ln2_g'], p['ln2_b']).reshape(B, L, D)


def kernel(x_prompt, x_sample, mem_prompt, mem_sample, w_in, w_short, hy_w1, hy_b1, hy_w2, hy_b2, hy_w3, hy_b3, hy_w4, hy_sin_freq, hy_dbias, hy_w_o, rw_w0, rw_w2, rw_a0, rw_a2, rw_g2, rw_k_k, rw_k_a, rw_r_k, rw_ln_w, rw_ln_b, rw_w_o, ca_w_kv, ca_w_o, w_out, ln1_g, ln1_b, moe_w_router, moe_w_gate, moe_w_up, moe_w_down, ln2_g, ln2_b):
    params = dict(w_in=w_in, w_short=w_short, hy_w1=hy_w1, hy_b1=hy_b1, hy_w2=hy_w2, hy_b2=hy_b2,
                  hy_w3=hy_w3, hy_b3=hy_b3, hy_w4=hy_w4, hy_sin_freq=hy_sin_freq, hy_dbias=hy_dbias,
                  hy_w_o=hy_w_o, rw_w0=rw_w0, rw_w2=rw_w2, rw_a0=rw_a0, rw_a2=rw_a2, rw_g2=rw_g2,
                  rw_k_k=rw_k_k, rw_k_a=rw_k_a, rw_r_k=rw_r_k, rw_ln_w=rw_ln_w, rw_ln_b=rw_ln_b,
                  rw_w_o=rw_w_o, ca_w_kv=ca_w_kv, ca_w_o=ca_w_o, w_out=w_out, ln1_g=ln1_g, ln1_b=ln1_b,
                  moe_w_router=moe_w_router, moe_w_gate=moe_w_gate, moe_w_up=moe_w_up,
                  moe_w_down=moe_w_down, ln2_g=ln2_g, ln2_b=ln2_b)
    bf = jnp.bfloat16
    y_prompt, y_sample = x_prompt, x_sample
    for i in range(DEPTH):
        lp = {name: arr[i] for name, arr in params.items()}
        wi = lp['w_in']
        lp['w_conv'] = jnp.pad(wi[:, :OFF_Q], ((0, 0), (0, N_LORA_PAD - N_LORA))).astype(bf)
        lp['w_rest'] = jnp.concatenate([wi[:, OFF_GATE:], wi[:, OFF_Q:OFF_GATE]], axis=1).astype(bf)
        lp['w_short_all'] = jnp.pad(lp['w_short'], ((0, 0), (0, N_LORA_PAD - N_LORA)))
        for name in ('hy_w_o', 'ca_w_o', 'ca_w_kv', 'w_out'):
            lp[name] = lp[name].astype(bf)
        lp['rw'] = rwkv_params(lp)
        lp['w_router_pad'] = jnp.pad(lp['moe_w_router'], ((0, 0), (0, LANES - N_EXPERTS))).astype(bf)
        for name in ('ln1_g', 'ln1_b', 'ln2_g', 'ln2_b'):
            lp[name] = lp[name].reshape(1, D_MODEL)
        y_prompt = _layer(y_prompt, mem_prompt, lp)
        y_sample = _layer(y_sample, mem_sample, lp)
    return (y_prompt, y_sample)
```

```python
import functools
import math

import jax
import jax.numpy as jnp
from jax import lax
from jax.experimental import pallas as pl
from jax.experimental.pallas import tpu as pltpu

D_MODEL = 2048
DEPTH = 1
D_HY = D_MODEL // 2
HY_ORDER = 2
HY_EMB = 33
HY_BANDS = (HY_EMB - 1) // 2
HY_FFN = 64
HY_FAST_DECAY_PCT = 0.3
HY_SLOW_DECAY_PCT = 1.5
HY_DECAY_TARGET = 1e-2
D_RW = D_MODEL // 2
RW_HEAD = 64
RW_HEADS = D_RW // RW_HEAD
RW_DECAY_LORA = 64
RW_AAA_LORA = 64
RW_GATE_LORA = 160
RW_GN_EPS = 64e-5
D_CA = D_MODEL // 2
CA_HEADS = 4
CA_HEAD = D_CA // CA_HEADS
N_BRANCH = 3
N_EXPERTS = 16
EC_CAPACITY = 2
D_EXPERT = 2 * D_MODEL
LN_EPS = 1e-5
DN_ALPHA = (2 * DEPTH) ** 0.25

OFF_HY = 0
OFF_RW = OFF_HY + 3 * D_HY
OFF_LORA = OFF_RW + 3 * D_RW
N_LORA = 2 * RW_DECAY_LORA + 2 * RW_AAA_LORA + RW_GATE_LORA
OFF_Q = OFF_LORA + N_LORA
OFF_GATE = OFF_Q + D_CA
N_IN = OFF_GATE + N_BRANCH * D_MODEL

LANES = 128
N_LORA_PAD = -(-N_LORA // LANES) * LANES
P_HY = 0
P_RW = P_HY + 3 * D_HY
P_LORA = P_RW + 3 * D_RW
P_Q = P_LORA + N_LORA_PAD
P_GATE = P_Q + D_CA
P_ALL = P_GATE + N_BRANCH * D_MODEL

VMEM_LIMIT = 56 * 1024 * 1024
SUBLANES = 8
CB = 512
N_CONV = P_Q
LORA_W = N_LORA_PAD


def _mm_kernel(a_ref, b_ref, o_ref):
    o_ref[...] = jnp.dot(a_ref[...].astype(jnp.bfloat16), b_ref[...].astype(jnp.bfloat16),
                         preferred_element_type=jnp.float32).astype(o_ref.dtype)


def _matmul(a, b, tm=1024, tn=512, out_dtype=jnp.float32):
    M, K = a.shape
    N = b.shape[1]
    tm = min(tm, M)
    tn = min(tn, N)
    assert M % tm == 0 and N % tn == 0, (M, N, tm, tn)
    return pl.pallas_call(
        _mm_kernel,
        grid=(M // tm, N // tn),
        in_specs=[pl.BlockSpec((tm, K), lambda i, j: (i, 0)),
                  pl.BlockSpec((K, tn), lambda i, j: (0, j))],
        out_specs=pl.BlockSpec((tm, tn), lambda i, j: (i, j)),
        out_shape=jax.ShapeDtypeStruct((M, N), out_dtype),
        compiler_params=pltpu.CompilerParams(dimension_semantics=("arbitrary", "arbitrary"),
                                             vmem_limit_bytes=VMEM_LIMIT),
        name="mm",
    )(a, b)


def _mm3(x, w, **kw):
    B, L, K = x.shape
    return _matmul(x.reshape(B * L, K), w, **kw).reshape(B, L, w.shape[1])


def _moe_kernel(x_ref, gv_ref, wg_ref, wu_ref, wd_ref, o_ref):
    @pl.when(pl.program_id(2) == 0)
    def _():
        o_ref[...] = jnp.zeros_like(o_ref)

    x = x_ref[0]
    hg = jnp.dot(x, wg_ref[0].astype(jnp.bfloat16), preferred_element_type=jnp.float32)
    hu = jnp.dot(x, wu_ref[0].astype(jnp.bfloat16), preferred_element_type=jnp.float32)
    h = (hg * jax.nn.sigmoid(hg) * hu).astype(jnp.bfloat16)
    o_ref[0] += jnp.dot(h, wd_ref[0].astype(jnp.bfloat16), preferred_element_type=jnp.float32)

    @pl.when(pl.program_id(2) == pl.num_programs(2) - 1)
    def _():
        o_ref[0] = o_ref[0] * jnp.tile(gv_ref[0], (1, o_ref.shape[2] // LANES))


def _moe_ffn(xe, gv, w_gate, w_up, w_down, tm=1024, tf=256):
    E, C, D = xe.shape
    F = w_gate.shape[2]
    tm = min(tm, C)
    return pl.pallas_call(
        _moe_kernel,
        grid=(E, C // tm, F // tf),
        in_specs=[pl.BlockSpec((1, tm, D), lambda e, i, f: (e, i, 0)),
                  pl.BlockSpec((1, tm, LANES), lambda e, i, f: (e, i, 0)),
                  pl.BlockSpec((1, D, tf), lambda e, i, f: (e, 0, f)),
                  pl.BlockSpec((1, D, tf), lambda e, i, f: (e, 0, f)),
                  pl.BlockSpec((1, tf, D), lambda e, i, f: (e, f, 0))],
        out_specs=pl.BlockSpec((1, tm, D), lambda e, i, f: (e, i, 0)),
        out_shape=jax.ShapeDtypeStruct((E, C, D), jnp.float32),
        compiler_params=pltpu.CompilerParams(dimension_semantics=("arbitrary",) * 3, vmem_limit_bytes=VMEM_LIMIT),
        name="moe_ffn",
    )(xe, gv, w_gate, w_up, w_down)


WKV_CHUNK = 64
HEAD = RW_HEAD
PAIR = 2 * HEAD


def _bdot(a, b):
    return jnp.dot(a.astype(jnp.bfloat16), b.astype(jnp.bfloat16), preferred_element_type=jnp.float32)


def _bdot_nt(a, b):
    return lax.dot_general(a.astype(jnp.bfloat16), b.astype(jnp.bfloat16), (((1,), (1,)), ((), ())),
                           preferred_element_type=jnp.float32)


def _bdot_tn(a, b):
    return lax.dot_general(a.astype(jnp.bfloat16), b.astype(jnp.bfloat16), (((0,), (0,)), ((), ())),
                           preferred_element_type=jnp.float32)


def _split3(x):
    f32, bf = jnp.float32, jnp.bfloat16
    hi = x.astype(bf)
    r1 = x - hi.astype(f32)
    mid = r1.astype(bf)
    lo = (r1 - mid.astype(f32)).astype(bf)
    return hi, mid, lo


def _wkv_chunk(rs, lws, ks, vs, kks, kkas, ss, rev):
    C = WKV_CHUNK
    f32, bf = jnp.float32, jnp.bfloat16
    n = len(rs)
    ri = lax.broadcasted_iota(jnp.int32, (C, C), 0)
    ci = lax.broadcasted_iota(jnp.int32, (C, C), 1)
    dd = jnp.where(rev == 0, ri - ci, ci - ri)
    tri = (dd >= 0).astype(bf)
    lane = lax.broadcasted_iota(jnp.int32, (C, PAIR), 1)
    first = lane < HEAD
    r2 = lax.broadcasted_iota(jnp.int32, (2 * C, 2 * C), 0)
    c2 = lax.broadcasted_iota(jnp.int32, (2 * C, 2 * C), 1)
    d2 = jnp.where(rev == 0, r2 - c2, c2 - r2)
    strict = d2 > 0
    incl = d2 >= 0
    eye = (r2 == c2).astype(f32)

    def stack(x):
        return jnp.concatenate([jnp.where(first, x, 0.0), jnp.where(first, 0.0, x)], axis=0)

    cums = []
    for lw in lws:
        hi, mid, lo = _split3(lw)
        cums.append(jnp.dot(tri, hi, preferred_element_type=f32) + jnp.dot(tri, mid, preferred_element_type=f32)
                    + jnp.dot(tri, lo, preferred_element_type=f32))
    galls = [jnp.exp(jnp.sum(lw, axis=0, keepdims=True)) for lw in lws]
    lhs, rhs, vS, a2k2 = [], [], [], []
    for i in range(n):
        cum, lw = cums[i], lws[i]
        ginv = jnp.exp(-cum)
        bt = stack(kks[i] * jnp.exp(cum - lw))
        rt = stack(rs[i] * jnp.exp(cum))
        at = stack(-(kkas[i] * ginv))
        kt = stack(ks[i] * ginv)
        lhs.append(jnp.concatenate([bt, rt], axis=0).astype(bf))
        rhs.append(jnp.concatenate([at, kt], axis=0).astype(bf))
        a2k2.append((jnp.concatenate([at, kt], axis=0) * galls[i]).astype(bf))
        vS.append(stack(vs[i]).astype(bf))
    gram = [_bdot_nt(lhs[i], rhs[i]) for i in range(n)]
    C2 = 2 * C
    mab = [jnp.where(strict, g[:C2, :C2], 0.0) for g in gram]
    mbk = [jnp.where(strict, g[:C2, C2:], 0.0).astype(bf) for g in gram]
    mrr = [jnp.concatenate([jnp.where(incl, g[C2:, :C2], 0.0), jnp.where(incl, g[C2:, C2:], 0.0)], axis=1).astype(bf)
           for g in gram]
    ts = [eye + m for m in mab]
    ps = [_bdot(m, m) for m in mab]
    for _ in range(4):
        xs = [_bdot(jnp.concatenate([t, p], axis=0), p) for t, p in zip(ts, ps)]
        ts = [t + x[:C2] for t, x in zip(ts, xs)]
        ps = [x[C2:] for x in xs]
    ts = [t + _bdot(t, p) for t, p in zip(ts, ps)]
    sB = [s.astype(bf) for s in ss]
    hs = [_bdot_nt(lhs[i], sB[i]) for i in range(n)]
    wS = [hs[i][:C2] + _bdot(mbk[i], vS[i]) for i in range(n)]
    uS = [_bdot(ts[i], wS[i]).astype(bf) for i in range(n)]
    uv = [jnp.concatenate([uS[i], vS[i]], axis=0) for i in range(n)]
    yS = [hs[i][C2:] + _bdot(mrr[i], uv[i]) for i in range(n)]
    ys = [y[:C] + y[C:] for y in yS]
    sn = [ss[i] * galls[i] + _bdot_tn(uv[i], a2k2[i]) for i in range(n)]
    return ys, sn


HALO = 2 * SUBLANES


def _projconv_kernel(x_ref, h_ref, w_ref, ws_ref, o_ref):
    w = w_ref[...]
    p = jnp.dot(x_ref[...], w, preferred_element_type=jnp.float32)
    ph = jnp.dot(h_ref[0], w, preferred_element_type=jnp.float32)
    tm = p.shape[0]
    rid = lax.broadcasted_iota(jnp.int32, (tm, 1), 0)
    prev = jnp.where(rid == 0, ph[SUBLANES - 1:SUBLANES, :], pltpu.roll(p, 1, 0))
    nxt = jnp.where(rid == tm - 1, ph[SUBLANES:SUBLANES + 1, :], pltpu.roll(p, tm - 1, 0))
    o_ref[...] = ws_ref[0:1, :] * prev + ws_ref[1:2, :] * p + ws_ref[2:3, :] * nxt


def _halo_rows(xb, tm):
    B, L, K = xb.shape
    nt = L // tm
    xr = xb.reshape(B, nt, tm, K)
    zero = jnp.zeros((B, 1, SUBLANES, K), xb.dtype)
    above = jnp.concatenate([zero, xr[:, :-1, tm - SUBLANES:]], axis=1)
    below = jnp.concatenate([xr[:, 1:, :SUBLANES], zero], axis=1)
    return jnp.concatenate([above, below], axis=2).reshape(B * nt, HALO, K)


def proj_conv(xb, w, ws, tm=1024, tn=512):
    B, L, K = xb.shape
    N = w.shape[1]
    tm = min(tm, L)
    assert L % tm == 0 and N % tn == 0
    halo = _halo_rows(xb, tm)
    T = B * L
    out = pl.pallas_call(
        _projconv_kernel,
        grid=(T // tm, N // tn),
        in_specs=[pl.BlockSpec((tm, K), lambda i, j: (i, 0)),
                  pl.BlockSpec((1, HALO, K), lambda i, j: (i, 0, 0)),
                  pl.BlockSpec((K, tn), lambda i, j: (0, j)),
                  pl.BlockSpec((3, tn), lambda i, j: (0, j))],
        out_specs=pl.BlockSpec((tm, tn), lambda i, j: (i, j)),
        out_shape=jax.ShapeDtypeStruct((T, N), jnp.float32),
        compiler_params=pltpu.CompilerParams(dimension_semantics=("arbitrary", "arbitrary"), vmem_limit_bytes=VMEM_LIMIT),
        name="proj_conv",
    )(xb.reshape(T, K), halo, w, ws)
    return out.reshape(B, L, N)


def _dft_mats(L):
    N = 2 * L
    f = lax.broadcasted_iota(jnp.int32, (L, L), 0)
    s = lax.broadcasted_iota(jnp.int32, (L, L), 1)
    ang = ((f * s) % N).astype(jnp.float32) * (2.0 * math.pi / N)
    a = jnp.cos(ang)
    sn = jnp.sin(ang)
    alt_s = (1 - 2 * (s % 2)).astype(jnp.float32)
    alt_f = (1 - 2 * (f % 2)).astype(jnp.float32)
    b = jnp.where(f == 0, alt_s, sn)
    b2 = jnp.where(s == 0, alt_f, sn)
    return a.astype(jnp.bfloat16), b.astype(jnp.bfloat16), b2.astype(jnp.bfloat16)


HY_BLOCK = 1024


def _hy_fwd_kernel(z_ref, a_ref, b_ref, kp_ref, kq_ref, yp_ref, yq_ref, zb_ref, *, tf, nb, P):
    j = pl.program_id(2)

    @pl.when(j == 0)
    def _():
        zb_ref[...] = z_ref[0].astype(jnp.bfloat16)

    a = a_ref[...]
    b = b_ref[...]
    ps = [jnp.dot(a, zb_ref[J * P:(J + 1) * P, :], preferred_element_type=jnp.float32) for J in range(nb)]
    qs = [jnp.dot(b, zb_ref[J * P:(J + 1) * P, :], preferred_element_type=jnp.float32) for J in range(nb)]
    row = lax.broadcasted_iota(jnp.int32, ps[0].shape, 0) + j * tf
    packed = row == 0
    for I in range(nb):
        yp = None
        yq = None
        for J in range(nb):
            kp = kp_ref[I - J + nb - 1]
            kq = kq_ref[I - J + nb - 1]
            kqq = kq * qs[J]
            tp = kp * ps[J] - jnp.where(packed, 0.0, kqq)
            tq = jnp.where(packed, kqq, kp * qs[J] + kq * ps[J])
            yp = tp if yp is None else yp + tp
            yq = tq if yq is None else yq + tq
        yp_ref[0, I] = yp.astype(yp_ref.dtype)
        yq_ref[0, I] = yq.astype(yq_ref.dtype)


def _hy_fwd(z, zcol, a, b, kp, kq, tf=256, tc=512):
    B, L, _ = z.shape
    nk, P, C = kp.shape
    nb = (nk + 1) // 2
    assert nb * P == L
    tf = min(tf, P)
    zspec = pl.BlockSpec((1, L, tc), lambda b_, c, j: (b_, 0, zcol + c))
    mspec = pl.BlockSpec((tf, P), lambda b_, c, j: (j, 0))
    kspec = pl.BlockSpec((nk, tf, tc), lambda b_, c, j: (0, j, c))
    ospec = pl.BlockSpec((1, nb, tf, tc), lambda b_, c, j: (b_, 0, j, c))
    return pl.pallas_call(
        functools.partial(_hy_fwd_kernel, tf=tf, nb=nb, P=P),
        grid=(B, C // tc, P // tf),
        in_specs=[zspec, mspec, mspec, kspec, kspec],
        out_specs=[ospec, ospec],
        out_shape=[jax.ShapeDtypeStruct((B, nb, P, C), jnp.bfloat16)] * 2,
        scratch_shapes=[pltpu.VMEM((L, tc), jnp.bfloat16)],
        compiler_params=pltpu.CompilerParams(dimension_semantics=("arbitrary",) * 3, vmem_limit_bytes=VMEM_LIMIT),
        name="hy_fwd",
    )(z, a, b, kp, kq)


def _hy_inv_kernel(yp_ref, yq_ref, a_ref, b2_ref, z_ref, x_ref, db_ref, o_ref):
    y = jnp.dot(a_ref[...], yp_ref[0, 0], preferred_element_type=jnp.float32)
    y += jnp.dot(b2_ref[...], yq_ref[0, 0], preferred_element_type=jnp.float32)
    o_ref[0] = (x_ref[0] * (y + z_ref[0] * db_ref[...])).astype(o_ref.dtype)


def _hy_inv(yp, yq, a, b2, zarr, zcol, xarr, xcol, db, out_dtype, tt=512, tc=512):
    B, nb, P, C = yp.shape
    tt = min(tt, P)
    nt = P // tt
    yspec = pl.BlockSpec((1, 1, P, tc), lambda b_, c, I, i: (b_, I, 0, c))
    mspec = pl.BlockSpec((tt, P), lambda b_, c, I, i: (i, 0))
    return pl.pallas_call(
        _hy_inv_kernel,
        grid=(B, C // tc, nb, nt),
        in_specs=[yspec, yspec, mspec, mspec,
                  pl.BlockSpec((1, tt, tc), lambda b_, c, I, i: (b_, I * nt + i, zcol + c)),
                  pl.BlockSpec((1, tt, tc), lambda b_, c, I, i: (b_, I * nt + i, xcol + c)),
                  pl.BlockSpec((1, tc), lambda b_, c, I, i: (0, c))],
        out_specs=pl.BlockSpec((1, tt, tc), lambda b_, c, I, i: (b_, I * nt + i, c)),
        out_shape=jax.ShapeDtypeStruct((B, nb * P, C), out_dtype),
        compiler_params=pltpu.CompilerParams(dimension_semantics=("arbitrary",) * 4, vmem_limit_bytes=VMEM_LIMIT),
        name="hy_inv",
    )(yp, yq, a, b2, zarr, xarr, db)


def _hy_spectra(filt, a, b, P, mm):
    L, n_ord, _, C = filt.shape
    nb = L // P
    N = 2 * P
    assert P % 2 == 0
    hcat = jnp.concatenate([filt[:, o, s].reshape(nb, P, C) for o in range(n_ord) for s in range(2)], axis=0)
    ns = hcat.shape[0]
    flat = jnp.transpose(hcat, (1, 0, 2)).reshape(P, ns * C).astype(jnp.bfloat16)
    cs = mm(a, flat).reshape(P, ns, C)
    sn = mm(b, flat).reshape(P, ns, C)
    row0 = (jnp.arange(P) == 0)[:, None]
    sign = (1 - 2 * (jnp.arange(P) % 2)).astype(jnp.float32)[:, None]
    wgt = jnp.where(row0, 1.0 / N, 2.0 / N)
    zero = jnp.zeros((P, C), jnp.float32)
    out = []
    for o in range(n_ord):
        def fwd(s, j, drop_first):
            if j >= nb:
                return zero, zero
            k = (o * 2 + s) * nb + j
            x0 = hcat[k, 0:1, :].astype(jnp.bfloat16).astype(jnp.float32) if drop_first else 0.0
            return cs[:, k] - x0, sn[:, k] - jnp.where(row0, x0, 0.0)

        def bwd(s, j):
            c, q = fwd(s, j, True)
            return sign * c, jnp.where(row0, q, -sign * q)

        def first(s, j):
            if j >= nb:
                return zero, zero
            k = (o * 2 + s) * nb + j
            x0 = jnp.broadcast_to(hcat[k, 0:1, :].astype(jnp.bfloat16).astype(jnp.float32), (P, C))
            return x0, jnp.where(row0, x0, 0.0)

        kps, kqs = [], []
        for e in range(-(nb - 1), nb):
            if e >= 1:
                cp, qp = fwd(0, e, False)
                cn, qn = bwd(0, e - 1)
            elif e == 0:
                cp, qp = fwd(0, 0, False)
                cn, qn = fwd(1, 0, True)
            else:
                g = -e
                c1, q1 = bwd(1, g - 1)
                c2, q2 = first(1, g)
                cp, qp = c1 + c2, q1 + q2
                cn, qn = fwd(1, g, True)
            kps.append((cp + cn) * wgt)
            kqs.append(jnp.where(row0, qp + qn, qp - qn) * wgt)
        out.append((jnp.stack(kps), jnp.stack(kqs)))
    return out


def hyena(sc, filt, dbias, mm, P=HY_BLOCK, tc=512):
    L = sc.shape[1]
    C = filt.shape[3]
    P = min(P, L)
    a, b, b2 = _dft_mats(P)
    (kp1, kq1), (kp2, kq2) = _hy_spectra(filt, a, b, P, mm)
    db = dbias.astype(jnp.float32)
    nbc = C // tc
    yp, yq = _hy_fwd(sc, 0, a, b, kp1, kq1, tc=tc)
    z2 = _hy_inv(yp, yq, a, b2, sc, 0, sc, nbc, db[0:1], jnp.float32, tc=tc)
    yp, yq = _hy_fwd(z2, 0, a, b, kp2, kq2, tc=tc)
    return _hy_inv(yp, yq, a, b2, z2, 0, sc, 2 * nbc, db[1:2], jnp.bfloat16, tc=tc)


def _split_dot(x, ones_bf16):
    hi = x.astype(jnp.bfloat16)
    lo = (x - hi.astype(jnp.float32)).astype(jnp.bfloat16)
    return (jnp.dot(hi, ones_bf16, preferred_element_type=jnp.float32)
            + jnp.dot(lo, ones_bf16, preferred_element_type=jnp.float32))


def _head_ones():
    r = lax.broadcasted_iota(jnp.int32, (PAIR, PAIR), 0) // HEAD
    c = lax.broadcasted_iota(jnp.int32, (PAIR, PAIR), 1) // HEAD
    return (r == c).astype(jnp.bfloat16)


def _group_sum(x, ones):
    return jnp.concatenate([_split_dot(x[:, p * PAIR:(p + 1) * PAIR], ones) for p in range(x.shape[1] // PAIR)], axis=1)


def _softplus(x):
    return jnp.maximum(x, 0.0) + jnp.log1p(jnp.exp(-jnp.abs(x)))


def _wkv_fused_kernel(r_ref, k_ref, v_ref, lo_ref, w0_ref, a0_ref, w2_ref, a2_ref, kk_ref, ka_ref, o_ref, h_ref, *, nbat):
    rev = pl.program_id(0)

    @pl.when(pl.program_id(2) == 0)
    def _():
        h_ref[...] = jnp.zeros_like(h_ref)

    f32 = jnp.float32
    C = WKV_CHUNK
    rows = nbat * C
    r = r_ref[...].reshape(rows, D_RW)
    k = k_ref[...].reshape(rows, D_RW)
    v = v_ref[...].reshape(rows, D_RW)
    lo = lo_ref[...].reshape(rows, LORA_W)
    xw = lo[:, 0:PAIR]
    xa = lo[:, PAIR:2 * PAIR]
    dw = jnp.dot(jnp.tanh(xw).astype(jnp.bfloat16), w2_ref[0], preferred_element_type=f32)
    lw = -jnp.exp(-_softplus(-(w0_ref[0] + dw)) - 0.5)
    a = jax.nn.sigmoid(a0_ref[0] + jnp.dot(xa.astype(jnp.bfloat16), a2_ref[0], preferred_element_type=f32))
    kq = k * kk_ref[...]
    ones = _head_ones()
    sq = (kq * kq).astype(jnp.bfloat16)
    ss = jnp.concatenate([jnp.dot(sq[:, p * PAIR:(p + 1) * PAIR], ones, preferred_element_type=f32)
                          for p in range(D_RW // PAIR)], axis=1)
    kk = kq / jnp.maximum(jnp.sqrt(ss), 1e-12)
    kdir = k * (1.0 + (a - 1.0) * ka_ref[...])
    kka = kk * a
    n = D_RW // PAIR
    parts = [(bi, slice(bi * C, (bi + 1) * C), slice(p * PAIR, (p + 1) * PAIR)) for bi in range(nbat) for p in range(n)]
    pick = lambda x: [x[rs, ls] for _, rs, ls in parts]
    ys, hn = _wkv_chunk(pick(r), pick(lw), pick(kdir), pick(v), pick(kk), pick(kka),
                        [h_ref[i] for i in range(len(parts))], rev)
    for i, (bi, _, ls) in enumerate(parts):
        o_ref[0, bi, :, ls] = ys[i]
        h_ref[i] = hn[i]


def wkv7_fused(sc, rcol, lcol, w0, a0, w2p, a2p, k_k, k_a, nbat=4):
    B, L, _ = sc.shape
    C = WKV_CHUNK
    nc = L // C
    D = D_RW
    nbat = min(nbat, B)
    assert L % C == 0 and B % nbat == 0
    cm = lambda d, c: c + d * (nc - 1 - 2 * c)
    col = lambda j: pl.BlockSpec((nbat, C, D), lambda d, b, c: (b, cm(d, c), j))
    vec2 = pl.BlockSpec((1, 1, D), lambda d, b, c: (d, 0, 0))
    mat2 = pl.BlockSpec((1, PAIR, D), lambda d, b, c: (d, 0, 0))
    vec = pl.BlockSpec((1, D), lambda d, b, c: (0, 0))
    return pl.pallas_call(
        functools.partial(_wkv_fused_kernel, nbat=nbat),
        grid=(2, B // nbat, nc),
        in_specs=[col(rcol), col(rcol + 1), col(rcol + 2),
                  pl.BlockSpec((nbat, C, LORA_W), lambda d, b, c: (b, cm(d, c), lcol)),
                  vec2, vec2, mat2, mat2, vec, vec],
        out_specs=pl.BlockSpec((1, nbat, C, D), lambda d, b, c: (d, b, cm(d, c), 0)),
        out_shape=jax.ShapeDtypeStruct((2, B, L, D), jnp.float32),
        scratch_shapes=[pltpu.VMEM((nbat * (D // PAIR), PAIR, PAIR), jnp.float32)],
        compiler_params=pltpu.CompilerParams(dimension_semantics=("arbitrary",) * 3, vmem_limit_bytes=VMEM_LIMIT),
        name="wkv7",
    )(sc, sc, sc, sc, w0, a0, w2p, a2p, k_k, k_a)


def _rw_post_kernel(y_ref, r_ref, k_ref, v_ref, lo_ref, a0_ref, a2_ref, ka_ref, rk_ref, lnw_ref, lnb_ref, g2_ref, wo_ref, o_ref):
    f32 = jnp.float32
    ones = _head_ones()
    y = y_ref[0, 0] + y_ref[1, 0]
    mu = _group_sum(y, ones) * (1.0 / HEAD)
    yc = y - mu
    var = _group_sum(yc * yc, ones) * (1.0 / HEAD)
    out = yc * lax.rsqrt(var + RW_GN_EPS) * lnw_ref[...] + lnb_ref[...]
    r = r_ref[0]
    k = k_ref[0]
    v = v_ref[0]
    xa = lo_ref[0, :, PAIR:2 * PAIR].astype(jnp.bfloat16)
    for d in range(2):
        a = jax.nn.sigmoid(a0_ref[d] + jnp.dot(xa, a2_ref[d], preferred_element_type=f32))
        kdir = k * (1.0 + (a - 1.0) * ka_ref[...])
        out = out + _group_sum(r * kdir * rk_ref[d], ones) * v
    g = jnp.dot(jax.nn.sigmoid(lo_ref[0]).astype(jnp.bfloat16), g2_ref[...], preferred_element_type=f32)
    o_ref[0] = jnp.dot((out * g).astype(jnp.bfloat16), wo_ref[...], preferred_element_type=f32).astype(o_ref.dtype)


def rw_post(y2, sc, rcol, lcol, a0, a2p, k_a, r_k, ln_w, ln_b, g2p, w_o, tm=256):
    _, B, L, D = y2.shape
    DM = w_o.shape[1]
    tm = min(tm, L)
    col = lambda j: pl.BlockSpec((1, tm, D), lambda b, i: (b, i, j))
    full = lambda shape: pl.BlockSpec(shape, lambda b, i: (0,) * len(shape))
    return pl.pallas_call(
        _rw_post_kernel,
        grid=(B, L // tm),
        in_specs=[pl.BlockSpec((2, 1, tm, D), lambda b, i: (0, b, i, 0)),
                  col(rcol), col(rcol + 1), col(rcol + 2),
                  pl.BlockSpec((1, tm, LORA_W), lambda b, i: (b, i, lcol)),
                  full((2, 1, D)), full((2, PAIR, D)), full((1, D)), full((2, 1, D)), full((1, D)), full((1, D)),
                  full((LORA_W, D)), full((D, DM))],
        out_specs=pl.BlockSpec((1, tm, DM), lambda b, i: (b, i, 0)),
        out_shape=jax.ShapeDtypeStruct((B, L, DM), jnp.bfloat16),
        compiler_params=pltpu.CompilerParams(dimension_semantics=("arbitrary",) * 2, vmem_limit_bytes=VMEM_LIMIT),
        name="rw_post",
    )(y2, sc, sc, sc, sc, a0, a2p, k_a, r_k, ln_w, ln_b, g2p, w_o)


def rwkv_params(p):
    f32, bf = jnp.float32, jnp.bfloat16
    z64 = jnp.zeros((64, D_RW), f32)
    w2p = jnp.stack([jnp.concatenate([p['rw_w2'][0], z64]), jnp.concatenate([z64, p['rw_w2'][1]])]).astype(bf)
    a2p = jnp.stack([jnp.concatenate([p['rw_a2'][0], z64]), jnp.concatenate([z64, p['rw_a2'][1]])]).astype(bf)
    g2p = jnp.zeros((LORA_W, D_RW), f32).at[256:256 + 160].set(p['rw_g2']).astype(bf)
    return dict(w0=p['rw_w0'].reshape(2, 1, D_RW), a0=p['rw_a0'].reshape(2, 1, D_RW), w2p=w2p, a2p=a2p, g2p=g2p,
                k_k=p['rw_k_k'].reshape(1, D_RW), k_a=p['rw_k_a'].reshape(1, D_RW), r_k=p['rw_r_k'].reshape(2, 1, D_RW),
                ln_w=p['rw_ln_w'].reshape(1, D_RW), ln_b=p['rw_ln_b'].reshape(1, D_RW), w_o=p['rw_w_o'].astype(bf))


def rwkv(sc, rcol, lcol, q):
    y2 = wkv7_fused(sc, rcol, lcol, q['w0'], q['a0'], q['w2p'], q['a2p'], q['k_k'], q['k_a'])
    return rw_post(y2, sc, rcol, lcol, q['a0'], q['a2p'], q['k_a'], q['r_k'], q['ln_w'], q['ln_b'], q['g2p'], q['w_o'])


def _attn_kernel(q_ref, k_ref, v_ref, wo_ref, o_ref):
    hp = pl.program_id(2)

    @pl.when(hp == 0)
    def _():
        o_ref[...] = jnp.zeros_like(o_ref)

    outs = []
    for h in range(CB // CA_HEAD):
        sl = slice(h * CA_HEAD, (h + 1) * CA_HEAD)
        q = q_ref[0, :, sl].astype(jnp.bfloat16)
        s = lax.dot_general(q, k_ref[0, :, sl].astype(jnp.bfloat16), (((1,), (1,)), ((), ())),
                            preferred_element_type=jnp.float32) * (CA_HEAD ** -0.5)
        s = s - jnp.max(s, axis=-1, keepdims=True)
        e = jnp.exp(s)
        p = e / jnp.sum(e, axis=-1, keepdims=True)
        outs.append(jnp.dot(p.astype(jnp.bfloat16), v_ref[0, :, sl].astype(jnp.bfloat16), preferred_element_type=jnp.float32))
    att = jnp.concatenate(outs, axis=1).astype(jnp.bfloat16)
    o_ref[0] += jnp.dot(att, wo_ref[...], preferred_element_type=jnp.float32)


def mem_attention(proj, qcol, kv, w_o, tm=512):
    B, L, _ = proj.shape
    M = kv.shape[1]
    DM = w_o.shape[1]
    nhp = w_o.shape[0] // CB
    tm = min(tm, L)
    return pl.pallas_call(
        _attn_kernel,
        grid=(B, L // tm, nhp),
        in_specs=[pl.BlockSpec((1, tm, CB), lambda b, i, h: (b, i, qcol + h)),
                  pl.BlockSpec((1, M, CB), lambda b, i, h: (b, 0, h)),
                  pl.BlockSpec((1, M, CB), lambda b, i, h: (b, 0, nhp + h)),
                  pl.BlockSpec((CB, DM), lambda b, i, h: (h, 0))],
        out_specs=pl.BlockSpec((1, tm, DM), lambda b, i, h: (b, i, 0)),
        out_shape=jax.ShapeDtypeStruct((B, L, DM), jnp.float32),
        compiler_params=pltpu.CompilerParams(dimension_semantics=("arbitrary",) * 3, vmem_limit_bytes=VMEM_LIMIT),
        name="mem_attn",
    )(proj, kv, kv, w_o)


def _merge_kernel(hy_ref, rw_ref, ca_ref, g0_ref, g1_ref, g2_ref, wo_ref, x_ref, lg_ref, lb_ref, wr_ref,
                  o_ref, ob_ref, lo_ref):
    f32 = jnp.float32
    m = (jax.nn.sigmoid(g0_ref[0].astype(f32)) * hy_ref[0] + jax.nn.sigmoid(g1_ref[0].astype(f32)) * rw_ref[0]
         + jax.nn.sigmoid(g2_ref[0].astype(f32)) * ca_ref[0])
    h = DN_ALPHA * x_ref[0] + jnp.dot(m.astype(jnp.bfloat16), wo_ref[...], preferred_element_type=f32)
    mu = jnp.mean(h, axis=-1, keepdims=True)
    hc = h - mu
    var = jnp.mean(hc * hc, axis=-1, keepdims=True)
    y = hc * lax.rsqrt(var + LN_EPS) * lg_ref[...] + lb_ref[...]
    o_ref[0] = DN_ALPHA * y
    yb = y.astype(jnp.bfloat16)
    ob_ref[0] = yb
    lo_ref[0] = jnp.dot(yb, wr_ref[...], preferred_element_type=f32)


def merge_ln(hy, rw, ca, pq, gcol, w_out, x, ln_g, ln_b, w_router_pad, tm=256):
    B, L, DM = x.shape
    tm = min(tm, L)
    row = pl.BlockSpec((1, tm, DM), lambda b, i: (b, i, 0))
    gate = lambda g: pl.BlockSpec((1, tm, DM), lambda b, i: (b, i, gcol + g))
    vec = pl.BlockSpec((1, DM), lambda b, i: (0, 0))
    return pl.pallas_call(
        _merge_kernel,
        grid=(B, L // tm),
        in_specs=[row, row, row, gate(0), gate(1), gate(2),
                  pl.BlockSpec((DM, DM), lambda b, i: (0, 0)), row, vec, vec,
                  pl.BlockSpec((DM, LANES), lambda b, i: (0, 0))],
        out_specs=[row, row, pl.BlockSpec((1, tm, LANES), lambda b, i: (b, i, 0))],
        out_shape=[jax.ShapeDtypeStruct((B, L, DM), jnp.float32), jax.ShapeDtypeStruct((B, L, DM), jnp.bfloat16),
                   jax.ShapeDtypeStruct((B, L, LANES), jnp.float32)],
        compiler_params=pltpu.CompilerParams(dimension_semantics=("arbitrary",) * 2, vmem_limit_bytes=VMEM_LIMIT),
        name="merge_ln",
    )(hy, rw, ca, pq, pq, pq, w_out, x, ln_g, ln_b, w_router_pad)


COMBINE_ROWS = 1024
COMBINE_UNROLL = 16


def _row_copy(y_ref, buf, row, r, sem, to_hbm):
    hbm = y_ref.at[pl.ds(row, 1)]
    vm = buf.at[pl.ds(r, 1)]
    return pltpu.make_async_copy(vm, hbm, sem) if to_hbm else pltpu.make_async_copy(hbm, vm, sem)


def _combine_kernel(idx_ref, out_ref, yin_ref, y_ref, buf, sem_in, sem_out, *, R):
    del yin_ref

    def start_in(g, c):
        for u in range(COMBINE_UNROLL):
            r = g * COMBINE_UNROLL + u
            _row_copy(y_ref, buf, idx_ref[0, 0, r], r, sem_in, False).start(priority=u % 2)
        return c

    def wait_in(r, c):
        _row_copy(y_ref, buf, idx_ref[0, 0, r], r, sem_in, False).wait()
        return c

    def start_out(g, c):
        for u in range(COMBINE_UNROLL):
            r = g * COMBINE_UNROLL + u
            _row_copy(y_ref, buf, idx_ref[0, 0, r], r, sem_out, True).start(priority=u % 2)
        return c

    def wait_out(r, c):
        _row_copy(y_ref, buf, idx_ref[0, 0, r], r, sem_out, True).wait()
        return c

    lax.fori_loop(0, R // COMBINE_UNROLL, start_in, 0)
    lax.fori_loop(0, R, wait_in, 0, unroll=COMBINE_UNROLL)
    buf[...] = buf[...] + out_ref[0]
    lax.fori_loop(0, R // COMBINE_UNROLL, start_out, 0)
    lax.fori_loop(0, R, wait_out, 0, unroll=COMBINE_UNROLL)


def moe_combine(idx, out, y0, R=COMBINE_ROWS):
    E, C, D = out.shape
    R = min(R, C)
    assert C % R == 0 and R % COMBINE_UNROLL == 0
    nch = C // R
    return pl.pallas_call(
        functools.partial(_combine_kernel, R=R),
        grid=(E, nch),
        in_specs=[pl.BlockSpec((1, 1, R), lambda e, c: (e * nch + c, 0, 0), memory_space=pltpu.SMEM),
                  pl.BlockSpec((1, R, D), lambda e, c: (e, c, 0)),
                  pl.BlockSpec(memory_space=pl.ANY)],
        out_specs=pl.BlockSpec(memory_space=pl.ANY),
        out_shape=jax.ShapeDtypeStruct(y0.shape, y0.dtype),
        scratch_shapes=[pltpu.VMEM((R, D), jnp.float32), pltpu.SemaphoreType.DMA, pltpu.SemaphoreType.DMA],
        input_output_aliases={2: 0},
        compiler_params=pltpu.CompilerParams(dimension_semantics=("arbitrary", "arbitrary"), vmem_limit_bytes=VMEM_LIMIT,
                                             has_side_effects=True),
        name="moe_combine",
    )(idx.reshape(E * nch, 1, R), out, y0)


def _ln_kernel(x_ref, g_ref, b_ref, o_ref):
    h = x_ref[...]
    mu = jnp.mean(h, axis=-1, keepdims=True)
    hc = h - mu
    var = jnp.mean(hc * hc, axis=-1, keepdims=True)
    o_ref[...] = hc * lax.rsqrt(var + LN_EPS) * g_ref[...] + b_ref[...]


def layer_norm(x, g, b, tm=512):
    T, DM = x.shape
    row = pl.BlockSpec((tm, DM), lambda i: (i, 0))
    vec = pl.BlockSpec((1, DM), lambda i: (0, 0))
    return pl.pallas_call(
        _ln_kernel, grid=(T // tm,), in_specs=[row, vec, vec], out_specs=row,
        out_shape=jax.ShapeDtypeStruct((T, DM), jnp.float32),
        compiler_params=pltpu.CompilerParams(dimension_semantics=("arbitrary",), vmem_limit_bytes=VMEM_LIMIT),
        name="final_ln",
    )(x, g, b)


def _hyena_filters(L, w1, b1, w2, b2, w3, b3, w4, sin_freq):
    f32 = jnp.float32
    t = jnp.linspace(0.0, 1.0, L, dtype=f32)[:, None]
    ang = 2.0 * math.pi * jnp.arange(L, dtype=f32)[:, None] / L
    bands = jnp.linspace(1e-4, HY_BANDS - 1, HY_BANDS, dtype=f32)[None]
    z = jnp.concatenate([t, jnp.cos(bands * ang), -jnp.sin(bands * ang)], axis=-1)
    sf = sin_freq.astype(f32)
    h = jnp.sin(sf[0] * (z @ w1.astype(f32) + b1.astype(f32)))
    h = jnp.sin(sf[1] * (h @ w2.astype(f32) + b2.astype(f32)))
    h = jnp.sin(sf[2] * (h @ w3.astype(f32) + b3.astype(f32)))
    h = (h @ w4.astype(f32)).reshape(L, HY_ORDER, 2, D_HY)
    deltas = jnp.abs(jnp.linspace(math.log(HY_DECAY_TARGET) / HY_SLOW_DECAY_PCT,
                                  math.log(HY_DECAY_TARGET) / HY_FAST_DECAY_PCT, D_HY, dtype=f32))
    window = jnp.exp(-t * deltas[None])
    return h * window[:, None, None, :]


def _expert_choice_moe(resid, xb, logits, w_gate, w_up, w_down):
    T, D = xb.shape
    cap = (EC_CAPACITY * T) // N_EXPERTS
    aff = jax.nn.softmax(logits.astype(jnp.float32), axis=-1)
    gval, idx = lax.top_k(aff.T, cap)
    out = _moe_ffn(xb[idx], jnp.broadcast_to(gval[..., None], gval.shape + (LANES,)), w_gate, w_up, w_down)
    return moe_combine(idx, out, resid)


def _layer(x, mem, p):
    B, L, D = x.shape
    xb = x.astype(jnp.bfloat16)
    sc = proj_conv(xb, p['w_conv'], p['w_short_all'])
    pq = _mm3(xb, p['w_rest'], out_dtype=jnp.bfloat16)
    filt = _hyena_filters(L, p['hy_w1'], p['hy_b1'], p['hy_w2'], p['hy_b2'], p['hy_w3'], p['hy_b3'],
                          p['hy_w4'], p['hy_sin_freq'])
    hy = _mm3(hyena(sc, filt, p['hy_dbias'], _matmul), p['hy_w_o'], out_dtype=jnp.bfloat16)
    rw = rwkv(sc, P_RW // D_RW, P_LORA // LORA_W, p['rw'])
    ca = mem_attention(pq, N_BRANCH * D_MODEL // CB, _mm3(mem, p['ca_w_kv']), p['ca_w_o'])
    xs, x1b, logits = merge_ln(hy, rw, ca, pq, 0, p['w_out'], x, p['ln1_g'], p['ln1_b'], p['w_router_pad'])
    h = _expert_choice_moe(xs.reshape(B * L, D), x1b.reshape(B * L, D), logits.reshape(B * L, LANES)[:, :N_EXPERTS],
                           p['moe_w_gate'], p['moe_w_up'], p['moe_w_down'])
    return layer_norm(h, p['ln2_g'], p['ln2_b']).reshape(B, L, D)


def kernel(x_prompt, x_sample, mem_prompt, mem_sample, w_in, w_short, hy_w1, hy_b1, hy_w2, hy_b2, hy_w3, hy_b3, hy_w4, hy_sin_freq, hy_dbias, hy_w_o, rw_w0, rw_w2, rw_a0, rw_a2, rw_g2, rw_k_k, rw_k_a, rw_r_k, rw_ln_w, rw_ln_b, rw_w_o, ca_w_kv, ca_w_o, w_out, ln1_g, ln1_b, moe_w_router, moe_w_gate, moe_w_up, moe_w_down, ln2_g, ln2_b):
    params = dict(w_in=w_in, w_short=w_short, hy_w1=hy_w1, hy_b1=hy_b1, hy_w2=hy_w2, hy_b2=hy_b2,
                  hy_w3=hy_w3, hy_b3=hy_b3, hy_w4=hy_w4, hy_sin_freq=hy_sin_freq, hy_dbias=hy_dbias,
                  hy_w_o=hy_w_o, rw_w0=rw_w0, rw_w2=rw_w2, rw_a0=rw_a0, rw_a2=rw_a2, rw_g2=rw_g2,
                  rw_k_k=rw_k_k, rw_k_a=rw_k_a, rw_r_k=rw_r_k, rw_ln_w=rw_ln_w, rw_ln_b=rw_ln_b,
                  rw_w_o=rw_w_o, ca_w_kv=ca_w_kv, ca_w_o=ca_w_o, w_out=w_out, ln1_g=ln1_g, ln1_b=ln1_b,
                  moe_w_router=moe_w_router, moe_w_gate=moe_w_gate, moe_w_up=moe_w_up,
                  moe_w_down=moe_w_down, ln2_g=ln2_g, ln2_b=ln2_b)
    bf = jnp.bfloat16
    y_prompt, y_sample = x_prompt, x_sample
    for i in range(DEPTH):
        lp = {name: arr[i] for name, arr in params.items()}
        wi = lp['w_in']
        lp['w_conv'] = jnp.pad(wi[:, :OFF_Q], ((0, 0), (0, N_LORA_PAD - N_LORA))).astype(bf)
        lp['w_rest'] = jnp.concatenate([wi[:, OFF_GATE:], wi[:, OFF_Q:OFF_GATE]], axis=1).astype(bf)
        lp['w_short_all'] = jnp.pad(lp['w_short'], ((0, 0), (0, N_LORA_PAD - N_LORA)))
        for name in ('hy_w_o', 'ca_w_o', 'ca_w_kv', 'w_out'):
            lp[name] = lp[name].astype(bf)
        lp['rw'] = rwkv_params(lp)
        lp['w_router_pad'] = jnp.pad(lp['moe_w_router'], ((0, 0), (0, LANES - N_EXPERTS))).astype(bf)
        for name in ('ln1_g', 'ln1_b', 'ln2_g', 'ln2_b'):
            lp[name] = lp[name].reshape(1, D_MODEL)
        y_prompt = _layer(y_prompt, mem_prompt, lp)
        y_sample = _layer(y_sample, mem_sample, lp)
    return (y_prompt, y_sample)
```
